```python
import math
import jax, jax.numpy as jnp
from jax import lax
import numpy as np

D_MODEL = 2048
BATCH = 8
SEQ = 4096
DEPTH = 4

N_MIXERS = 2
Q_BLOCK = 128
D_FF = 4 * D_MODEL
RMS_EPS = 1e-6

A_HEADS = 32
A_QK_DIM = 128
A_V_DIM = 128
A_Q_RANK = 512
A_KV_RANK = 256
IDX_HEADS = 16
IDX_DIM = 128
IDX_TOPK = 256

B_HEADS = 16
B_HEAD_DIM = D_MODEL // B_HEADS

REL_BUCKETS = 32
REL_MAX_DIST = 128

N_A_LAYERS = (DEPTH + 1) // 2
N_B_LAYERS = DEPTH // 2
A_IN_COLS = A_Q_RANK + A_KV_RANK + IDX_DIM + IDX_HEADS
B_IN_COLS = 3 * B_HEADS * B_HEAD_DIM

kernel_name = "hybrid_dsa_stickbreaking_adaln_trunk"


def rmsnorm(x, g):
    xf = x.astype(jnp.float32)
    y = xf * lax.rsqrt(jnp.mean(xf * xf, axis=-1, keepdims=True) + RMS_EPS)
    return (y * g.astype(jnp.float32)).astype(x.dtype)


def layernorm(x, g):
    xf = x.astype(jnp.float32)
    mu = jnp.mean(xf, axis=-1, keepdims=True)
    var = jnp.mean(jnp.square(xf - mu), axis=-1, keepdims=True)
    return ((xf - mu) * lax.rsqrt(var + RMS_EPS) * g.astype(jnp.float32)).astype(x.dtype)


def modulate(h, shift, scale):
    return h * (1 + scale[:, None, :]) + shift[:, None, :]


def t5_bucket(dist):
    max_exact = REL_BUCKETS // 2
    d = jnp.maximum(dist, 1).astype(jnp.float32)
    large = max_exact + (jnp.log(d / max_exact) / math.log(REL_MAX_DIST / max_exact)
                         * (REL_BUCKETS - max_exact)).astype(jnp.int32)
    large = jnp.minimum(large, REL_BUCKETS - 1)
    return jnp.where(dist < max_exact, dist, large)


def dsa_attention(h, w_in, q_norm, kv_norm, idx_k_norm, w_qb, w_idx_qb, w_uk, w_uv, w_out, rel_bias):
    B, S, _ = h.shape
    k_top = min(IDX_TOPK, S // 4)
    proj = h @ w_in
    c1 = A_Q_RANK
    c2 = c1 + A_KV_RANK
    c3 = c2 + IDX_DIM
    q_lat, c_kv, k_idx, w_idx = jnp.split(proj, [c1, c2, c3], axis=-1)
    q_lat = rmsnorm(q_lat, q_norm)
    c_kv = rmsnorm(c_kv, kv_norm)
    k_idx = layernorm(k_idx, idx_k_norm)
    q = (q_lat @ w_qb).reshape(B, S, A_HEADS, A_QK_DIM)
    q_idx = (q_lat @ w_idx_qb).reshape(B, S, IDX_HEADS, IDX_DIM)
    w_idx = w_idx * (IDX_HEADS ** -0.5 * IDX_DIM ** -0.5)
    scale = A_QK_DIM ** -0.5
    key_pos = jnp.arange(S)

    def block(i):
        q0 = i * Q_BLOCK
        t = q0 + jnp.arange(Q_BLOCK)
        qb = lax.dynamic_slice_in_dim(q, q0, Q_BLOCK, axis=1)
        qib = lax.dynamic_slice_in_dim(q_idx, q0, Q_BLOCK, axis=1)
        wb = lax.dynamic_slice_in_dim(w_idx, q0, Q_BLOCK, axis=1)
        dots = jnp.einsum('bqhd,bsd->bqsh', qib, k_idx)
        score = jnp.einsum('bqsh,bqh->bqs', jax.nn.relu(dots), wb).astype(jnp.float32)
        score = jnp.where(key_pos[None, None, :] <= t[None, :, None], score, -jnp.inf)
        _, idx = lax.top_k(score, k_top)
        valid = idx <= t[None, :, None]
        kv_sel = jax.vmap(lambda kv, ix: kv[ix])(c_kv, idx)
        q_abs = jnp.einsum('bqhd,hcd->bqhc', qb, w_uk)
        logits = jnp.einsum('bqhc,bqkc->bqhk', q_abs, kv_sel).astype(jnp.float32) * scale
        bucket = t5_bucket(jnp.maximum(t[None, :, None] - idx, 0))
        logits = logits + jnp.moveaxis(rel_bias[bucket], -1, 2).astype(jnp.float32)
        logits = jnp.where(valid[:, :, None, :], logits, -jnp.inf)
        probs = jax.nn.softmax(logits, axis=-1).astype(h.dtype)
        o_lat = jnp.einsum('bqhk,bqkc->bqhc', probs, kv_sel)
        return jnp.einsum('bqhc,hcv->bqhv', o_lat, w_uv)

    out = lax.map(block, jnp.arange(S // Q_BLOCK))
    out = jnp.moveaxis(out, 0, 1).reshape(B, S, A_HEADS * A_V_DIM)
    return out @ w_out


def stick_breaking_attention(h, w_in, w_out):
    B, S, _ = h.shape
    q, k, v = jnp.split(h @ w_in, 3, axis=-1)
    q = q.reshape(B, S, B_HEADS, B_HEAD_DIM)
    k = k.reshape(B, S, B_HEADS, B_HEAD_DIM)
    v = v.reshape(B, S, B_HEADS, B_HEAD_DIM)
    scale = B_HEAD_DIM ** -0.5
    key_pos = jnp.arange(S)

    def block(i):
        q0 = i * Q_BLOCK
        t = q0 + jnp.arange(Q_BLOCK)
        qb = lax.dynamic_slice_in_dim(q, q0, Q_BLOCK, axis=1)
        z = jnp.einsum('bqhd,bshd->bhqs', qb, k).astype(jnp.float32) * scale
        strict = key_pos[None, :] < t[:, None]
        log_beta = jax.nn.log_sigmoid(z)
        log_rest = jnp.where(strict, log_beta - z, 0.0)
        later = lax.cumsum(log_rest, axis=3, reverse=True) - log_rest
        weights = jnp.where(strict, jnp.exp(log_beta + later), 0.0).astype(h.dtype)
        return jnp.einsum('bhqs,bshd->bqhd', weights, v)

    out = lax.map(block, jnp.arange(S // Q_BLOCK))
    out = jnp.moveaxis(out, 0, 1).reshape(B, S, B_HEADS * B_HEAD_DIM)
    return out @ w_out


def setup_inputs(seed: int = 0) -> dict:
    key = jax.random.key(seed)
    ks = jax.random.split(key, 20)

    def w(k, shape, fan_in):
        return jax.random.normal(k, shape, jnp.float32) * (fan_in ** -0.5)

    def gain(k, shape):
        return 1.0 + 0.05 * jax.random.normal(k, shape, jnp.float32)

    D = D_MODEL
    return {
        "x": jax.random.normal(ks[0], (BATCH, SEQ, D), jnp.float32),
        "c": jax.random.normal(ks[1], (BATCH, D), jnp.float32),
        "rel_bias": 0.5 * jax.random.normal(ks[2], (REL_BUCKETS, A_HEADS), jnp.float32),
        "ada_w": w(ks[3], (DEPTH, D, 6 * D), D),
        "ada_b": 0.01 * jax.random.normal(ks[4], (DEPTH, 6 * D), jnp.float32),
        "norm_g": gain(ks[5], (DEPTH, 4, D)),
        "a_w_in": w(ks[6], (N_A_LAYERS, D, A_IN_COLS), D),
        "a_q_norm": gain(ks[7], (N_A_LAYERS, A_Q_RANK)),
        "a_kv_norm": gain(ks[8], (N_A_LAYERS, A_KV_RANK)),
        "a_idx_k_norm": gain(ks[9], (N_A_LAYERS, IDX_DIM)),
        "a_w_qb": w(ks[10], (N_A_LAYERS, A_Q_RANK, A_HEADS * A_QK_DIM), A_Q_RANK),
        "a_w_idx_qb": w(ks[11], (N_A_LAYERS, A_Q_RANK, IDX_HEADS * IDX_DIM), A_Q_RANK),
        "a_w_uk": w(ks[12], (N_A_LAYERS, A_HEADS, A_KV_RANK, A_QK_DIM), A_KV_RANK),
        "a_w_uv": w(ks[13], (N_A_LAYERS, A_HEADS, A_KV_RANK, A_V_DIM), A_KV_RANK),
        "a_w_out": w(ks[14], (N_A_LAYERS, A_HEADS * A_V_DIM, D), A_HEADS * A_V_DIM),
        "b_w_in": w(ks[15], (N_B_LAYERS, D, B_IN_COLS), D),
        "b_w_out": w(ks[16], (N_B_LAYERS, B_HEADS * B_HEAD_DIM, D), B_HEADS * B_HEAD_DIM),
        "mlp_w1": w(ks[17], (DEPTH, D, D_FF), D),
        "mlp_w2": w(ks[18], (DEPTH, D_FF, D), D_FF),
    }


def reference(x, c, rel_bias, ada_w, ada_b, norm_g, a_w_in, a_q_norm, a_kv_norm, a_idx_k_norm,
              a_w_qb, a_w_idx_qb, a_w_uk, a_w_uv, a_w_out, b_w_in, b_w_out, mlp_w1, mlp_w2):
    mod = jnp.einsum('bd,lde->lbe', jax.nn.silu(c), ada_w) + ada_b[:, None, :]
    for layer in range(DEPTH):
        shift_m, scale_m, gate_m, shift_f, scale_f, gate_f = jnp.split(mod[layer], 6, axis=-1)
        j = layer // N_MIXERS
        h = modulate(rmsnorm(x, norm_g[layer, 0]), shift_m, scale_m)
        if layer % N_MIXERS == 0:
            y = dsa_attention(h, a_w_in[j], a_q_norm[j], a_kv_norm[j], a_idx_k_norm[j],
                              a_w_qb[j], a_w_idx_qb[j], a_w_uk[j], a_w_uv[j], a_w_out[j], rel_bias)
        else:
            y = stick_breaking_attention(h, b_w_in[j], b_w_out[j])
        x = x + gate_m[:, None, :] * rmsnorm(y, norm_g[layer, 1])
        h = modulate(rmsnorm(x, norm_g[layer, 2]), shift_f, scale_f)
        y = jnp.square(jax.nn.relu(h @ mlp_w1[layer])) @ mlp_w2[layer]
        x = x + gate_f[:, None, :] * rmsnorm(y, norm_g[layer, 3])
    return x
```

```python
import functools
import math

import numpy as np
import jax
import jax.numpy as jnp
from jax import lax
from jax.experimental import pallas as pl
from jax.experimental.pallas import tpu as pltpu

F32 = jnp.float32
BF16 = jnp.bfloat16

RMS_EPS = 1e-6
B_HEADS = 16
IDX_TOPK = 256
REL_MAX_DIST = 128

LANES = 128
V7X_VMEM_LIMIT_BYTES = 56 * 1024 * 1024

MASK_NEG = -1e30
M_INIT = -1e29
INT_MIN = int(np.iinfo(np.int32).min)

NT_DIMS = (((1,), (1,)), ((), ()))


def _params(semantics):
    return pltpu.CompilerParams(dimension_semantics=semantics,
                                vmem_limit_bytes=V7X_VMEM_LIMIT_BYTES)


def _rms(x, g):
    return x * lax.rsqrt(jnp.mean(x * x, axis=-1, keepdims=True) + RMS_EPS) * g


def _normmod(x, g, shift, scale):
    return _rms(x, g) * (1.0 + scale) + shift


def _t5_bucket_thresholds(num_buckets):
    max_exact = num_buckets // 2

    def bucket(d):
        if d < max_exact:
            return d
        large = max_exact + int(math.log(max(d, 1) / max_exact)
                                / math.log(REL_MAX_DIST / max_exact) * (num_buckets - max_exact))
        return min(large, num_buckets - 1)

    thr, d = [], 0
    for b in range(num_buckets):
        while bucket(d) < b:
            d += 1
        thr.append(d)
    return tuple(thr)


def _ada_kernel(c_ref, w_ref, b_ref, o_ref):
    c = c_ref[...]
    s = c * jax.nn.sigmoid(c)
    o_ref[0] = jnp.dot(s.astype(BF16), w_ref[0].astype(BF16), preferred_element_type=F32) + b_ref[0]


def _ada_mod(c, ada_w, ada_b):
    L, D, E = ada_w.shape
    B = c.shape[0]
    tn = min(E, 1024)
    return pl.pallas_call(
        _ada_kernel,
        out_shape=jax.ShapeDtypeStruct((L, B, E), F32),
        grid=(L, E // tn),
        in_specs=[pl.BlockSpec((B, D), lambda l, j: (0, 0)),
                  pl.BlockSpec((1, D, tn), lambda l, j: (l, 0, j)),
                  pl.BlockSpec((1, 1, tn), lambda l, j: (l, 0, j))],
        out_specs=pl.BlockSpec((1, B, tn), lambda l, j: (l, 0, j)),
        compiler_params=_params(("parallel", "parallel")),
        name="ada_mod",
    )(c, ada_w, ada_b.reshape(L, 1, E))


def _in_proj_kernel(x_ref, g_ref, sh_ref, sc_ref, w_ref, o_ref, h_ref):
    @pl.when(pl.program_id(2) == 0)
    def _():
        h_ref[...] = _normmod(x_ref[0], g_ref[...], sh_ref[0], sc_ref[0]).astype(BF16)

    o_ref[0] = jnp.dot(h_ref[...], w_ref[...], preferred_element_type=F32).astype(o_ref.dtype)


def _in_proj(x, g, modr, shift_row, scale_row, w, out_dtype):
    B, S, D = x.shape
    n_out = w.shape[1]
    tm = min(S, 512)
    tn = min(n_out, 1024)
    return pl.pallas_call(
        _in_proj_kernel,
        out_shape=jax.ShapeDtypeStruct((B, S, n_out), out_dtype),
        grid=(B, S // tm, n_out // tn),
        in_specs=[pl.BlockSpec((1, tm, D), lambda b, i, j: (b, i, 0)),
                  pl.BlockSpec((1, D), lambda b, i, j: (0, 0)),
                  pl.BlockSpec((1, 1, D), lambda b, i, j: (shift_row(b), 0, 0)),
                  pl.BlockSpec((1, 1, D), lambda b, i, j: (scale_row(b), 0, 0)),
                  pl.BlockSpec((D, tn), lambda b, i, j: (0, j))],
        out_specs=pl.BlockSpec((1, tm, tn), lambda b, i, j: (b, i, j)),
        scratch_shapes=[pltpu.VMEM((tm, D), BF16)],
        compiler_params=_params(("parallel", "parallel", "arbitrary")),
        name="in_proj",
    )(x, g.reshape(1, D), modr, modr, w)


def _out_proj_kernel(a_ref, w_ref, x_ref, g_ref, gate_ref, o_ref, acc_ref):
    k = pl.program_id(2)

    @pl.when(k == 0)
    def _():
        acc_ref[...] = jnp.zeros_like(acc_ref)

    acc_ref[...] += jnp.dot(a_ref[0], w_ref[...], preferred_element_type=F32)

    @pl.when(k == pl.num_programs(2) - 1)
    def _():
        o_ref[0] = x_ref[0] + gate_ref[0] * _rms(acc_ref[...], g_ref[...])


def _out_proj(a, w, x, g, modr, gate_row):
    B, S, D = x.shape
    kin = a.shape[2]
    tm = min(S, 512)
    tk = min(kin, 2048)
    return pl.pallas_call(
        _out_proj_kernel,
        out_shape=jax.ShapeDtypeStruct((B, S, D), F32),
        grid=(B, S // tm, kin // tk),
        in_specs=[pl.BlockSpec((1, tm, tk), lambda b, i, k: (b, i, k)),
                  pl.BlockSpec((tk, D), lambda b, i, k: (k, 0)),
                  pl.BlockSpec((1, tm, D), lambda b, i, k: (b, i, 0)),
                  pl.BlockSpec((1, D), lambda b, i, k: (0, 0)),
                  pl.BlockSpec((1, 1, D), lambda b, i, k: (gate_row(b), 0, 0))],
        out_specs=pl.BlockSpec((1, tm, D), lambda b, i, k: (b, i, 0)),
        scratch_shapes=[pltpu.VMEM((tm, D), F32)],
        compiler_params=_params(("parallel", "parallel", "arbitrary")),
        name="out_proj",
    )(a, w, x, g.reshape(1, D), modr)


def _mlp_kernel(x_ref, g_in_ref, sh_ref, sc_ref, w1_ref, w2_ref, g_out_ref, gate_ref, o_ref, h_ref, acc_ref):
    f = pl.program_id(2)

    @pl.when(f == 0)
    def _():
        h_ref[...] = _normmod(x_ref[0], g_in_ref[...], sh_ref[0], sc_ref[0]).astype(BF16)
        acc_ref[...] = jnp.zeros_like(acc_ref)

    a = jnp.dot(h_ref[...], w1_ref[...], preferred_element_type=F32)
    a = jnp.square(jnp.maximum(a, 0.0)).astype(BF16)
    acc_ref[...] += jnp.dot(a, w2_ref[...], preferred_element_type=F32)

    @pl.when(f == pl.num_programs(2) - 1)
    def _():
        o_ref[0] = x_ref[0] + gate_ref[0] * _rms(acc_ref[...], g_out_ref[...])


def _mlp(x, g_in, g_out, modr, shift_row, scale_row, gate_row, w1, w2):
    B, S, D = x.shape
    dff = w1.shape[1]
    tm = min(S, 512)
    tf = min(dff, 512)
    return pl.pallas_call(
        _mlp_kernel,
        out_shape=jax.ShapeDtypeStruct((B, S, D), F32),
        grid=(B, S // tm, dff // tf),
        in_specs=[pl.BlockSpec((1, tm, D), lambda b, i, f: (b, i, 0)),
                  pl.BlockSpec((1, D), lambda b, i, f: (0, 0)),
                  pl.BlockSpec((1, 1, D), lambda b, i, f: (shift_row(b), 0, 0)),
                  pl.BlockSpec((1, 1, D), lambda b, i, f: (scale_row(b), 0, 0)),
                  pl.BlockSpec((D, tf), lambda b, i, f: (0, f)),
                  pl.BlockSpec((tf, D), lambda b, i, f: (f, 0)),
                  pl.BlockSpec((1, D), lambda b, i, f: (0, 0)),
                  pl.BlockSpec((1, 1, D), lambda b, i, f: (gate_row(b), 0, 0))],
        out_specs=pl.BlockSpec((1, tm, D), lambda b, i, f: (b, i, 0)),
        scratch_shapes=[pltpu.VMEM((tm, D), BF16), pltpu.VMEM((tm, D), F32)],
        compiler_params=_params(("parallel", "parallel", "arbitrary")),
        name="mlp",
    )(x, g_in.reshape(1, D), modr, modr, w1, w2, g_out.reshape(1, D), modr)


def _dsa_proj_kernel(x_ref, g_ref, sh_ref, sc_ref, wq_ref, wkv_ref, wki_ref, wwi_ref,
                     qn_ref, kvn_ref, kin_ref, wqb_ref, wiqb_ref, wuk_ref,
                     qa_ref, qi_ref, ckv_ref, ki_ref, wi_ref, *, w_idx_scale):
    h = _normmod(x_ref[0], g_ref[...], sh_ref[0], sc_ref[0]).astype(BF16)
    q_lat = _rms(jnp.dot(h, wq_ref[...], preferred_element_type=F32), qn_ref[...]).astype(BF16)
    ckv_ref[0] = _rms(jnp.dot(h, wkv_ref[...], preferred_element_type=F32), kvn_ref[...]).astype(BF16)

    ki = jnp.dot(h, wki_ref[...], preferred_element_type=F32)
    mu = jnp.mean(ki, axis=-1, keepdims=True)
    var = jnp.mean(jnp.square(ki - mu), axis=-1, keepdims=True)
    ki_ref[0] = ((ki - mu) * lax.rsqrt(var + RMS_EPS) * kin_ref[...]).astype(BF16)

    n_ih = wi_ref.shape[2]
    wi = jnp.dot(h, wwi_ref[...], preferred_element_type=F32)
    wi_ref[0] = wi[:, :n_ih] * w_idx_scale

    n_heads, dh = wuk_ref.shape[0], wuk_ref.shape[1]
    q = jnp.dot(q_lat, wqb_ref[...], preferred_element_type=F32)
    for hd in range(n_heads):
        qh = q[:, hd * dh:(hd + 1) * dh].astype(BF16)
        qa_ref[0, hd] = jnp.dot(qh, wuk_ref[hd], preferred_element_type=F32).astype(BF16)

    di = qi_ref.shape[3]
    qidx = jnp.dot(q_lat, wiqb_ref[...], preferred_element_type=F32)
    for hd in range(qi_ref.shape[1]):
        qi_ref[0, hd] = qidx[:, hd * di:(hd + 1) * di].astype(BF16)


def _dsa_proj(x, g, modr, shift_row, scale_row, wq, wkv, wki, wwi, q_norm, kv_norm, idx_k_norm,
              w_qb, w_idx_qb, w_uk_t, n_idx_heads):
    B, S, D = x.shape
    rq, rkv, di = wq.shape[1], wkv.shape[1], wki.shape[1]
    n_heads, dh, _ = w_uk_t.shape
    tm = min(S, 256)
    const = lambda *shape: pl.BlockSpec(shape, lambda b, i: (0,) * len(shape))
    kern = functools.partial(_dsa_proj_kernel, w_idx_scale=float(n_idx_heads ** -0.5 * di ** -0.5))
    return pl.pallas_call(
        kern,
        out_shape=(jax.ShapeDtypeStruct((B, n_heads, S, rkv), BF16),
                   jax.ShapeDtypeStruct((B, n_idx_heads, S, di), BF16),
                   jax.ShapeDtypeStruct((B, S, rkv), BF16),
                   jax.ShapeDtypeStruct((B, S, di), BF16),
                   jax.ShapeDtypeStruct((B, S, n_idx_heads), F32)),
        grid=(B, S // tm),
        in_specs=[pl.BlockSpec((1, tm, D), lambda b, i: (b, i, 0)),
                  const(1, D),
                  pl.BlockSpec((1, 1, D), lambda b, i: (shift_row(b), 0, 0)),
                  pl.BlockSpec((1, 1, D), lambda b, i: (scale_row(b), 0, 0)),
                  const(D, rq), const(D, rkv), const(D, di), const(D, LANES),
                  const(1, rq), const(1, rkv), const(1, di),
                  const(rq, n_heads * dh), const(rq, n_idx_heads * di), const(n_heads, dh, rkv)],
        out_specs=(pl.BlockSpec((1, n_heads, tm, rkv), lambda b, i: (b, 0, i, 0)),
                   pl.BlockSpec((1, n_idx_heads, tm, di), lambda b, i: (b, 0, i, 0)),
                   pl.BlockSpec((1, tm, rkv), lambda b, i: (b, i, 0)),
                   pl.BlockSpec((1, tm, di), lambda b, i: (b, i, 0)),
                   pl.BlockSpec((1, tm, n_idx_heads), lambda b, i: (b, i, 0))),
        compiler_params=_params(("parallel", "parallel")),
        name="dsa_proj",
    )(x, g.reshape(1, D), modr, modr, wq, wkv, wki, wwi,
      q_norm.reshape(1, rq), kv_norm.reshape(1, rkv), idx_k_norm.reshape(1, di),
      w_qb, w_idx_qb, w_uk_t)


def _mono_key(score):
    score = jnp.where(score == 0.0, 0.0, score)
    bits = lax.bitcast_convert_type(score, jnp.int32)
    return bits ^ ((bits >> 31) & 0x7FFFFFFF)


def _indexer_kernel(qi_ref, ki_ref, w_ref, o_ref, key_ref, *, k_top, chunk):
    i = pl.program_id(1)
    n_ih, tq, di = qi_ref.shape[1:]
    n_pieces_total = o_ref.shape[2]
    per_chunk = chunk // LANES

    q2 = qi_ref[0].reshape(n_ih * tq, di)
    w = w_ref[0]
    wcols = [w[:, h:h + 1] for h in range(n_ih)]
    n_chunks = (i * tq + tq - 1) // chunk + 1
    row_t = i * tq + lax.broadcasted_iota(jnp.int32, (tq, chunk), 0)
    col_l = lax.broadcasted_iota(jnp.int32, (tq, chunk), 1)

    def chunk_body(c, carry):
        kc = ki_ref[0, pl.ds(pl.multiple_of(c * chunk, chunk), chunk), :]
        d = lax.dot_general(q2, kc, NT_DIMS, preferred_element_type=F32)
        sc = jnp.zeros((tq, chunk), F32)
        for h in range(n_ih):
            sc = sc + jnp.maximum(d[h * tq:(h + 1) * tq], 0.0) * wcols[h]
        key = jnp.where(c * chunk + col_l <= row_t, _mono_key(sc), INT_MIN)
        for k in range(per_chunk):
            key_ref[c * per_chunk + k] = key[:, k * LANES:(k + 1) * LANES]
        return carry

    lax.fori_loop(0, n_chunks, chunk_body, 0)
    n_pieces = n_chunks * per_chunk

    def count(pred):
        def body(p, cnt):
            return cnt + jnp.where(pred(key_ref[p]), 1.0, 0.0)
        cnt = lax.fori_loop(0, n_pieces, body, jnp.zeros((tq, LANES), F32))
        return jnp.broadcast_to(jnp.sum(cnt, axis=1, keepdims=True), (tq, LANES))

    kf = float(k_top)
    zero = jnp.zeros((tq, LANES), jnp.int32)
    prefix = jnp.where(count(lambda k: k >= zero) >= kf, zero, INT_MIN)

    def bit_body(b, prefix):
        cand = prefix | jnp.left_shift(jnp.int32(1), 30 - b)
        return jnp.where(count(lambda k: k >= cand) >= kf, cand, prefix)

    tau = lax.fori_loop(0, 31, bit_body, prefix)
    n_ge = count(lambda k: k >= tau)
    tied = jnp.max(n_ge) > kf

    @pl.when(jnp.logical_not(tied))
    def _():
        def body(p, carry):
            key = key_ref[p]
            o_ref[0, 0, p] = jnp.where((key >= tau) & (key != INT_MIN), 0.0, MASK_NEG)
            return carry
        lax.fori_loop(0, n_pieces, body, 0)

    @pl.when(tied)
    def _():
        room = kf - count(lambda k: k > tau)
        before = (lax.broadcasted_iota(jnp.int32, (LANES, LANES), 0)
                  < lax.broadcasted_iota(jnp.int32, (LANES, LANES), 1)).astype(BF16)

        def body(p, seen):
            key = key_ref[p]
            tie = (key == tau) & (key != INT_MIN)
            tie_f = jnp.where(tie, 1.0, 0.0)
            rank = seen + jnp.dot(tie_f.astype(BF16), before, preferred_element_type=F32)
            o_ref[0, 0, p] = jnp.where((key > tau) | (tie & (rank < room)), 0.0, MASK_NEG)
            return seen + jnp.broadcast_to(jnp.sum(tie_f, axis=1, keepdims=True), (tq, LANES))
        lax.fori_loop(0, n_pieces, body, jnp.zeros((tq, LANES), F32))

    def fill(p, carry):
        o_ref[0, 0, p] = jnp.full((tq, LANES), MASK_NEG, F32)
        return carry
    lax.fori_loop(n_pieces, n_pieces_total, fill, 0)


def _indexer_mask(q_idx, k_idx, w_idx, k_top):
    B, n_ih, S, di = q_idx.shape
    tq = LANES
    chunk = min(S, 512)
    n_pieces = S // LANES
    kern = functools.partial(_indexer_kernel, k_top=k_top, chunk=chunk)
    return pl.pallas_call(
        kern,
        out_shape=jax.ShapeDtypeStruct((B, S // tq, n_pieces, tq, LANES), F32),
        grid=(B, S // tq),
        in_specs=[pl.BlockSpec((1, n_ih, tq, di), lambda b, i: (b, 0, i, 0)),
                  pl.BlockSpec((1, S, di), lambda b, i: (b, 0, 0)),
                  pl.BlockSpec((1, tq, n_ih), lambda b, i: (b, i, 0))],
        out_specs=pl.BlockSpec((1, 1, n_pieces, tq, LANES), lambda b, i: (b, i, 0, 0, 0)),
        scratch_shapes=[pltpu.VMEM((n_pieces, tq, LANES), jnp.int32)],
        compiler_params=_params(("parallel", "parallel")),
        name="indexer_mask",
    )(q_idx, k_idx, w_idx)


def _bias_tile_kernel(rb_ref, o_ref, *, thresholds):
    kind = pl.program_id(0)
    h = pl.program_id(1)
    nb = len(thresholds)
    d = (lax.broadcasted_iota(jnp.int32, (LANES, LANES), 0)
         - lax.broadcasted_iota(jnp.int32, (LANES, LANES), 1) + kind * LANES)
    d = jnp.maximum(d, 0)
    far = rb_ref[nb - 1, h]
    val = jnp.full((LANES, LANES), rb_ref[0, h] - far, F32)
    for b in range(1, nb):
        val = jnp.where(d >= thresholds[b], rb_ref[b, h] - far, val)
    o_ref[0, 0] = jnp.where(kind == 2, 0.0, val)


def _bias_tiles(rel_bias):
    nb, n_heads = rel_bias.shape
    thresholds = _t5_bucket_thresholds(nb)
    assert thresholds[nb - 1] <= LANES + 1, "bias must saturate within one 128-key block"
    return pl.pallas_call(
        functools.partial(_bias_tile_kernel, thresholds=thresholds),
        out_shape=jax.ShapeDtypeStruct((3, n_heads, LANES, LANES), F32),
        grid=(3, n_heads),
        in_specs=[pl.BlockSpec(memory_space=pltpu.SMEM)],
        out_specs=pl.BlockSpec((1, 1, LANES, LANES), lambda k, h: (k, h, 0, 0)),
        compiler_params=_params(("parallel", "parallel")),
        name="bias_tiles",
    )(rel_bias)


def _dsa_attn_kernel(qa_ref, ckv_ref, mk_ref, tab_ref, wuv_ref, o_ref, acc_ref, m_ref, l_ref,
                     *, scale, group, tk):
    i = pl.program_id(1)
    n_heads, tq, c = qa_ref.shape[1:]
    rows = group * tq
    dv = wuv_ref.shape[2]

    acc_ref[...] = jnp.zeros_like(acc_ref)
    m_ref[...] = jnp.full_like(m_ref, M_INIT)
    l_ref[...] = jnp.zeros_like(l_ref)

    def tile_kind(dist):
        return jnp.where(dist == 0, 0, jnp.where(dist == 1, 1, 2))

    def kv_body(j, carry):
        ckv = ckv_ref[0, pl.ds(pl.multiple_of(j * tk, tk), tk), :]
        p0 = (tk // LANES) * j
        madd = jnp.concatenate([mk_ref[0, 0, p0 + k] for k in range(tk // LANES)], axis=1)

        def head_group(g, with_bias):
            r0 = pl.multiple_of(g * rows, rows)
            q = qa_ref[0, pl.ds(g * group, group)].reshape(rows, c)
            s = lax.dot_general(q, ckv, NT_DIMS, preferred_element_type=F32) * scale
            s = s.reshape(group, tq, tk) + madd[None]
            if with_bias:
                s = s + jnp.concatenate(
                    [tab_ref[tile_kind(i - (p0 + k)), pl.ds(g * group, group)] for k in range(tk // LANES)],
                    axis=2)
            s = s.reshape(rows, tk)
            m_old = m_ref[pl.ds(r0, rows), :]
            m_new = jnp.maximum(m_old, jnp.max(s, axis=1, keepdims=True))
            alpha = jnp.exp(m_old - m_new)
            p = jnp.exp(s - m_new)
            l_ref[pl.ds(r0, rows), :] = alpha * l_ref[pl.ds(r0, rows), :] + jnp.sum(p, axis=1, keepdims=True)
            acc_ref[pl.ds(r0, rows), :] = (alpha * acc_ref[pl.ds(r0, rows), :]
                                           + jnp.dot(p.astype(BF16), ckv, preferred_element_type=F32))
            m_ref[pl.ds(r0, rows), :] = m_new

        near = (i - p0) <= tk // LANES

        @pl.when(near)
        def _():
            lax.fori_loop(0, n_heads // group, lambda g, cr: (head_group(g, True), cr)[1], 0)

        @pl.when(jnp.logical_not(near))
        def _():
            lax.fori_loop(0, n_heads // group, lambda g, cr: (head_group(g, False), cr)[1], 0)

        return carry

    n_kb = (i * tq + tq - 1) // tk + 1
    lax.fori_loop(0, n_kb, kv_body, 0)

    for h in range(n_heads):
        o_lat = (acc_ref[h * tq:(h + 1) * tq, :] / l_ref[h * tq:(h + 1) * tq, :]).astype(BF16)
        o_ref[0, :, h * dv:(h + 1) * dv] = jnp.dot(o_lat, wuv_ref[h], preferred_element_type=F32).astype(BF16)


def _dsa_attention(q_abs, c_kv, mask, bias_tab, w_uv, scale):
    B, n_heads, S, c = q_abs.shape
    dv = w_uv.shape[2]
    tq = LANES
    tk = min(S, 2 * LANES)
    group = min(n_heads, 8)
    n_pieces = S // LANES
    kern = functools.partial(_dsa_attn_kernel, scale=scale, group=group, tk=tk)
    return pl.pallas_call(
        kern,
        out_shape=jax.ShapeDtypeStruct((B, S, n_heads * dv), BF16),
        grid=(B, S // tq),
        in_specs=[pl.BlockSpec((1, n_heads, tq, c), lambda b, i: (b, 0, i, 0)),
                  pl.BlockSpec((1, S, c), lambda b, i: (b, 0, 0)),
                  pl.BlockSpec((1, 1, n_pieces, tq, LANES), lambda b, i: (b, i, 0, 0, 0)),
                  pl.BlockSpec((3, n_heads, LANES, LANES), lambda b, i: (0, 0, 0, 0)),
                  pl.BlockSpec((n_heads, c, dv), lambda b, i: (0, 0, 0))],
        out_specs=pl.BlockSpec((1, tq, n_heads * dv), lambda b, i: (b, i, 0)),
        scratch_shapes=[pltpu.VMEM((n_heads * tq, c), F32),
                        pltpu.VMEM((n_heads * tq, 1), F32),
                        pltpu.VMEM((n_heads * tq, 1), F32)],
        compiler_params=_params(("parallel", "parallel")),
        name="dsa_attention",
    )(q_abs, c_kv, mask, bias_tab, w_uv)


def _sb_attn_kernel(q_ref, k_ref, v_ref, o_ref, acc_ref, rest_ref, *, scale, tk):
    i = pl.program_id(2)
    tq = q_ref.shape[1]
    q = q_ref[0]
    row = lax.broadcasted_iota(jnp.int32, (tq, tk), 0)
    col = lax.broadcasted_iota(jnp.int32, (tq, tk), 1)
    after = (lax.broadcasted_iota(jnp.int32, (tk, tk), 0)
             > lax.broadcasted_iota(jnp.int32, (tk, tk), 1)).astype(BF16)

    acc_ref[...] = jnp.zeros_like(acc_ref)
    rest_ref[...] = jnp.zeros_like(rest_ref)

    def kv_body(jj, carry):
        j = i - jj
        k0 = pl.multiple_of(j * tk, tk)
        ks = k_ref[0, pl.ds(k0, tk), :]
        vs = v_ref[0, pl.ds(k0, tk), :]
        z = lax.dot_general(q, ks, NT_DIMS, preferred_element_type=F32) * scale
        log_beta = jnp.minimum(z, 0.0) - jnp.log1p(jnp.exp(-jnp.abs(z)))
        strict = (j * tk + col) < (i * tq + row)
        log_rest = jnp.where(strict, log_beta - z, 0.0)
        hi = log_rest.astype(BF16)
        lo = (log_rest - hi.astype(F32)).astype(BF16)
        later = (jnp.dot(hi, after, preferred_element_type=F32)
                 + jnp.dot(lo, after, preferred_element_type=F32)
                 + rest_ref[...])
        wgt = jnp.where(strict, jnp.exp(log_beta + later), 0.0)
        acc_ref[...] += jnp.dot(wgt.astype(BF16), vs, preferred_element_type=F32)
        rest_ref[...] += jnp.sum(log_rest, axis=1, keepdims=True)
        return carry

    lax.fori_loop(0, i * tq // tk + 1, kv_body, 0)
    o_ref[0] = acc_ref[...].astype(o_ref.dtype)


def _sb_attention(qkv, n_heads):
    B, S, three_d = qkv.shape
    dh = three_d // (3 * n_heads)
    tq = tk = min(S, 2 * LANES)
    kern = functools.partial(_sb_attn_kernel, scale=float(dh ** -0.5), tk=tk)
    return pl.pallas_call(
        kern,
        out_shape=jax.ShapeDtypeStruct((B, S, n_heads * dh), BF16),
        grid=(B, n_heads, S // tq),
        in_specs=[pl.BlockSpec((1, tq, dh), lambda b, h, i: (b, i, h)),
                  pl.BlockSpec((1, S, dh), lambda b, h, i: (b, 0, n_heads + h)),
                  pl.BlockSpec((1, S, dh), lambda b, h, i: (b, 0, 2 * n_heads + h))],
        out_specs=pl.BlockSpec((1, tq, dh), lambda b, h, i: (b, i, h)),
        scratch_shapes=[pltpu.VMEM((tq, dh), F32), pltpu.VMEM((tq, 1), F32)],
        compiler_params=_params(("parallel", "parallel", "parallel")),
        name="sb_attention",
    )(qkv, qkv, qkv)


def kernel(x, c, rel_bias, ada_w, ada_b, norm_g, a_w_in, a_q_norm, a_kv_norm, a_idx_k_norm, a_w_qb, a_w_idx_qb,
           a_w_uk, a_w_uv, a_w_out, b_w_in, b_w_out, mlp_w1, mlp_w2):
    B, S, D = x.shape
    depth = ada_w.shape[0]
    n_mixers = 2

    mod = _ada_mod(c, ada_w, ada_b)
    modr = mod.reshape(depth * B * 6, 1, D)

    def mod_row(layer, chunk):
        return lambda b: (layer * B + b) * 6 + chunk

    rq, rkv, di = a_q_norm.shape[1], a_kv_norm.shape[1], a_idx_k_norm.shape[1]
    n_a_heads, _, a_qk_dim = a_w_uk.shape[1:]
    n_idx_heads = a_w_idx_qb.shape[2] // di
    k_top = min(IDX_TOPK, S // 4)
    bias_tab = _bias_tiles(rel_bias)

    for layer in range(depth):
        j = layer // n_mixers
        g = norm_g[layer]
        if layer % n_mixers == 0:
            w_in = a_w_in[j].astype(BF16)
            wwi = jnp.pad(w_in[:, rq + rkv + di:], ((0, 0), (0, LANES - n_idx_heads)))
            q_abs, q_idx, c_kv, k_idx, w_idx = _dsa_proj(
                x, g[0], modr, mod_row(layer, 0), mod_row(layer, 1),
                w_in[:, :rq], w_in[:, rq:rq + rkv], w_in[:, rq + rkv:rq + rkv + di], wwi,
                a_q_norm[j], a_kv_norm[j], a_idx_k_norm[j],
                a_w_qb[j].astype(BF16), a_w_idx_qb[j].astype(BF16),
                jnp.swapaxes(a_w_uk[j], 1, 2).astype(BF16), n_idx_heads)
            mask = _indexer_mask(q_idx, k_idx, w_idx, k_top)
            y = _dsa_attention(q_abs, c_kv, mask, bias_tab, a_w_uv[j].astype(BF16), float(a_qk_dim ** -0.5))
            w_out = a_w_out[j].astype(BF16)
        else:
            qkv = _in_proj(x, g[0], modr, mod_row(layer, 0), mod_row(layer, 1), b_w_in[j].astype(BF16), BF16)
            y = _sb_attention(qkv, B_HEADS)
            w_out = b_w_out[j].astype(BF16)
        x = _out_proj(y, w_out, x, g[1], modr, mod_row(layer, 2))
        x = _mlp(x, g[2], g[3], modr, mod_row(layer, 3), mod_row(layer, 4), mod_row(layer, 5),
                 mlp_w1[layer].astype(BF16), mlp_w2[layer].astype(BF16))
    return x
```

```python
import functools
import math

import numpy as np
import jax
import jax.numpy as jnp
from jax import lax
from jax.experimental import pallas as pl
from jax.experimental.pallas import tpu as pltpu

F32 = jnp.float32
BF16 = jnp.bfloat16

RMS_EPS = 1e-6
B_HEADS = 16
IDX_TOPK = 256
REL_MAX_DIST = 128

LANES = 128
V7X_VMEM_LIMIT_BYTES = 56 * 1024 * 1024

MASK_NEG = -1e30
M_INIT = -1e29
INT_MIN = int(np.iinfo(np.int32).min)
LOG2E = math.log2(math.e)

NT_DIMS = (((1,), (1,)), ((), ()))


def _params(semantics):
    return pltpu.CompilerParams(dimension_semantics=semantics,
                                vmem_limit_bytes=V7X_VMEM_LIMIT_BYTES)


def _rms(x, g):
    return x * lax.rsqrt(jnp.mean(x * x, axis=-1, keepdims=True) + RMS_EPS) * g


def _normmod(x, g, shift, scale):
    return _rms(x, g) * (1.0 + scale) + shift


def _t5_bucket_thresholds(num_buckets):
    max_exact = num_buckets // 2

    def bucket(d):
        if d < max_exact:
            return d
        large = max_exact + int(math.log(max(d, 1) / max_exact)
                                / math.log(REL_MAX_DIST / max_exact) * (num_buckets - max_exact))
        return min(large, num_buckets - 1)

    thr, d = [], 0
    for b in range(num_buckets):
        while bucket(d) < b:
            d += 1
        thr.append(d)
    return tuple(thr)


def _ada_kernel(c_ref, w_ref, b_ref, o_ref):
    c = c_ref[...]
    s = c * jax.nn.sigmoid(c)
    o_ref[0] = jnp.dot(s.astype(BF16), w_ref[0].astype(BF16), preferred_element_type=F32) + b_ref[0]


def _ada_mod(c, ada_w, ada_b):
    L, D, E = ada_w.shape
    B = c.shape[0]
    tn = min(E, 1024)
    return pl.pallas_call(
        _ada_kernel,
        out_shape=jax.ShapeDtypeStruct((L, B, E), F32),
        grid=(L, E // tn),
        in_specs=[pl.BlockSpec((B, D), lambda l, j: (0, 0)),
                  pl.BlockSpec((1, D, tn), lambda l, j: (l, 0, j)),
                  pl.BlockSpec((1, 1, tn), lambda l, j: (l, 0, j))],
        out_specs=pl.BlockSpec((1, B, tn), lambda l, j: (l, 0, j)),
        compiler_params=_params(("parallel", "parallel")),
        name="ada_mod",
    )(c, ada_w, ada_b.reshape(L, 1, E))


def _in_proj_kernel(x_ref, g_ref, sh_ref, sc_ref, w_ref, o_ref, h_ref):
    @pl.when(pl.program_id(2) == 0)
    def _():
        h_ref[...] = _normmod(x_ref[0], g_ref[...], sh_ref[0], sc_ref[0]).astype(BF16)

    o_ref[0] = jnp.dot(h_ref[...], w_ref[...], preferred_element_type=F32).astype(o_ref.dtype)


def _in_proj(x, g, modr, shift_row, scale_row, w, out_dtype):
    B, S, D = x.shape
    n_out = w.shape[1]
    tm = min(S, 512)
    tn = min(n_out, 1024)
    return pl.pallas_call(
        _in_proj_kernel,
        out_shape=jax.ShapeDtypeStruct((B, S, n_out), out_dtype),
        grid=(B, S // tm, n_out // tn),
        in_specs=[pl.BlockSpec((1, tm, D), lambda b, i, j: (b, i, 0)),
                  pl.BlockSpec((1, D), lambda b, i, j: (0, 0)),
                  pl.BlockSpec((1, 1, D), lambda b, i, j: (shift_row(b), 0, 0)),
                  pl.BlockSpec((1, 1, D), lambda b, i, j: (scale_row(b), 0, 0)),
                  pl.BlockSpec((D, tn), lambda b, i, j: (0, j))],
        out_specs=pl.BlockSpec((1, tm, tn), lambda b, i, j: (b, i, j)),
        scratch_shapes=[pltpu.VMEM((tm, D), BF16)],
        compiler_params=_params(("parallel", "parallel", "arbitrary")),
        name="in_proj",
    )(x, g.reshape(1, D), modr, modr, w)


def _out_proj_kernel(a_ref, w_ref, x_ref, g_ref, gate_ref, o_ref, acc_ref):
    k = pl.program_id(2)

    @pl.when(k == 0)
    def _():
        acc_ref[...] = jnp.zeros_like(acc_ref)

    acc_ref[...] += jnp.dot(a_ref[0], w_ref[...], preferred_element_type=F32)

    @pl.when(k == pl.num_programs(2) - 1)
    def _():
        o_ref[0] = x_ref[0] + gate_ref[0] * _rms(acc_ref[...], g_ref[...])


def _out_proj(a, w, x, g, modr, gate_row):
    B, S, D = x.shape
    kin = a.shape[2]
    tm = min(S, 512)
    tk = min(kin, 2048)
    return pl.pallas_call(
        _out_proj_kernel,
        out_shape=jax.ShapeDtypeStruct((B, S, D), F32),
        grid=(B, S // tm, kin // tk),
        in_specs=[pl.BlockSpec((1, tm, tk), lambda b, i, k: (b, i, k)),
                  pl.BlockSpec((tk, D), lambda b, i, k: (k, 0)),
                  pl.BlockSpec((1, tm, D), lambda b, i, k: (b, i, 0)),
                  pl.BlockSpec((1, D), lambda b, i, k: (0, 0)),
                  pl.BlockSpec((1, 1, D), lambda b, i, k: (gate_row(b), 0, 0))],
        out_specs=pl.BlockSpec((1, tm, D), lambda b, i, k: (b, i, 0)),
        scratch_shapes=[pltpu.VMEM((tm, D), F32)],
        compiler_params=_params(("parallel", "parallel", "arbitrary")),
        name="out_proj",
    )(a, w, x, g.reshape(1, D), modr)


def _mlp_kernel(x_ref, g_in_ref, sh_ref, sc_ref, w1_ref, w2_ref, g_out_ref, gate_ref, o_ref, h_ref, acc_ref):
    f = pl.program_id(2)

    @pl.when(f == 0)
    def _():
        h_ref[...] = _normmod(x_ref[0], g_in_ref[...], sh_ref[0], sc_ref[0]).astype(BF16)
        acc_ref[...] = jnp.zeros_like(acc_ref)

    a = jnp.dot(h_ref[...], w1_ref[...], preferred_element_type=F32)
    a = jnp.square(jnp.maximum(a, 0.0)).astype(BF16)
    acc_ref[...] += jnp.dot(a, w2_ref[...], preferred_element_type=F32)

    @pl.when(f == pl.num_programs(2) - 1)
    def _():
        o_ref[0] = x_ref[0] + gate_ref[0] * _rms(acc_ref[...], g_out_ref[...])


def _mlp(x, g_in, g_out, modr, shift_row, scale_row, gate_row, w1, w2):
    B, S, D = x.shape
    dff = w1.shape[1]
    tm = min(S, 512)
    tf = min(dff, 512)
    return pl.pallas_call(
        _mlp_kernel,
        out_shape=jax.ShapeDtypeStruct((B, S, D), F32),
        grid=(B, S // tm, dff // tf),
        in_specs=[pl.BlockSpec((1, tm, D), lambda b, i, f: (b, i, 0)),
                  pl.BlockSpec((1, D), lambda b, i, f: (0, 0)),
                  pl.BlockSpec((1, 1, D), lambda b, i, f: (shift_row(b), 0, 0)),
                  pl.BlockSpec((1, 1, D), lambda b, i, f: (scale_row(b), 0, 0)),
                  pl.BlockSpec((D, tf), lambda b, i, f: (0, f)),
                  pl.BlockSpec((tf, D), lambda b, i, f: (f, 0)),
                  pl.BlockSpec((1, D), lambda b, i, f: (0, 0)),
                  pl.BlockSpec((1, 1, D), lambda b, i, f: (gate_row(b), 0, 0))],
        out_specs=pl.BlockSpec((1, tm, D), lambda b, i, f: (b, i, 0)),
        scratch_shapes=[pltpu.VMEM((tm, D), BF16), pltpu.VMEM((tm, D), F32)],
        compiler_params=_params(("parallel", "parallel", "arbitrary")),
        name="mlp",
    )(x, g_in.reshape(1, D), modr, modr, w1, w2, g_out.reshape(1, D), modr)


def _dsa_proj_kernel(x_ref, g_ref, sh_ref, sc_ref, wq_ref, wkv_ref, wki_ref, wwi_ref,
                     qn_ref, kvn_ref, kin_ref, wqb_ref, wiqb_ref, wuk_ref,
                     qa_ref, qi_ref, ckv_ref, ki_ref, wi_ref, *, w_idx_scale):
    h = _normmod(x_ref[0], g_ref[...], sh_ref[0], sc_ref[0]).astype(BF16)
    q_lat = _rms(jnp.dot(h, wq_ref[...], preferred_element_type=F32), qn_ref[...]).astype(BF16)
    ckv_ref[0] = _rms(jnp.dot(h, wkv_ref[...], preferred_element_type=F32), kvn_ref[...]).astype(BF16)

    ki = jnp.dot(h, wki_ref[...], preferred_element_type=F32)
    mu = jnp.mean(ki, axis=-1, keepdims=True)
    var = jnp.mean(jnp.square(ki - mu), axis=-1, keepdims=True)
    ki_ref[0] = ((ki - mu) * lax.rsqrt(var + RMS_EPS) * kin_ref[...]).astype(BF16)

    n_ih = wi_ref.shape[2]
    wi = jnp.dot(h, wwi_ref[...], preferred_element_type=F32)
    wi_ref[0] = wi[:, :n_ih] * w_idx_scale

    n_heads, dh = wuk_ref.shape[0], wuk_ref.shape[1]
    q = jnp.dot(q_lat, wqb_ref[...], preferred_element_type=F32)
    for hd in range(n_heads):
        qh = q[:, hd * dh:(hd + 1) * dh].astype(BF16)
        qa_ref[0, hd] = jnp.dot(qh, wuk_ref[hd], preferred_element_type=F32).astype(BF16)

    di = qi_ref.shape[3]
    qidx = jnp.dot(q_lat, wiqb_ref[...], preferred_element_type=F32)
    for hd in range(qi_ref.shape[1]):
        qi_ref[0, hd] = qidx[:, hd * di:(hd + 1) * di].astype(BF16)


def _dsa_proj(x, g, modr, shift_row, scale_row, wq, wkv, wki, wwi, q_norm, kv_norm, idx_k_norm,
              w_qb, w_idx_qb, w_uk_t, n_idx_heads):
    B, S, D = x.shape
    rq, rkv, di = wq.shape[1], wkv.shape[1], wki.shape[1]
    n_heads, dh, _ = w_uk_t.shape
    tm = min(S, 256)
    const = lambda *shape: pl.BlockSpec(shape, lambda b, i: (0,) * len(shape))
    kern = functools.partial(_dsa_proj_kernel, w_idx_scale=float(n_idx_heads ** -0.5 * di ** -0.5))
    return pl.pallas_call(
        kern,
        out_shape=(jax.ShapeDtypeStruct((B, n_heads, S, rkv), BF16),
                   jax.ShapeDtypeStruct((B, n_idx_heads, S, di), BF16),
                   jax.ShapeDtypeStruct((B, S, rkv), BF16),
                   jax.ShapeDtypeStruct((B, S, di), BF16),
                   jax.ShapeDtypeStruct((B, S, n_idx_heads), F32)),
        grid=(B, S // tm),
        in_specs=[pl.BlockSpec((1, tm, D), lambda b, i: (b, i, 0)),
                  const(1, D),
                  pl.BlockSpec((1, 1, D), lambda b, i: (shift_row(b), 0, 0)),
                  pl.BlockSpec((1, 1, D), lambda b, i: (scale_row(b), 0, 0)),
                  const(D, rq), const(D, rkv), const(D, di), const(D, LANES),
                  const(1, rq), const(1, rkv), const(1, di),
                  const(rq, n_heads * dh), const(rq, n_idx_heads * di), const(n_heads, dh, rkv)],
        out_specs=(pl.BlockSpec((1, n_heads, tm, rkv), lambda b, i: (b, 0, i, 0)),
                   pl.BlockSpec((1, n_idx_heads, tm, di), lambda b, i: (b, 0, i, 0)),
                   pl.BlockSpec((1, tm, rkv), lambda b, i: (b, i, 0)),
                   pl.BlockSpec((1, tm, di), lambda b, i: (b, i, 0)),
                   pl.BlockSpec((1, tm, n_idx_heads), lambda b, i: (b, i, 0))),
        compiler_params=_params(("parallel", "parallel")),
        name="dsa_proj",
    )(x, g.reshape(1, D), modr, modr, wq, wkv, wki, wwi,
      q_norm.reshape(1, rq), kv_norm.reshape(1, rkv), idx_k_norm.reshape(1, di),
      w_qb, w_idx_qb, w_uk_t)


def _mono_key(score):
    score = jnp.where(score == 0.0, 0.0, score)
    bits = lax.bitcast_convert_type(score, jnp.int32)
    return bits ^ ((bits >> 31) & 0x7FFFFFFF)


def _indexer_kernel(qi_ref, ki_ref, w_ref, o_ref, key_ref, *, k_top, chunk):
    i = pl.program_id(1)
    n_ih, tq, di = qi_ref.shape[1:]
    n_pieces_total = o_ref.shape[2]
    per_chunk = chunk // LANES

    q2 = qi_ref[0].reshape(n_ih * tq, di)
    w = w_ref[0]
    wcols = [w[:, h:h + 1] for h in range(n_ih)]
    n_chunks = (i * tq + tq - 1) // chunk + 1
    row_t = i * tq + lax.broadcasted_iota(jnp.int32, (tq, chunk), 0)
    col_l = lax.broadcasted_iota(jnp.int32, (tq, chunk), 1)

    def chunk_body(c, carry):
        kc = ki_ref[0, pl.ds(pl.multiple_of(c * chunk, chunk), chunk), :]
        d = lax.dot_general(q2, kc, NT_DIMS, preferred_element_type=F32)
        sc = jnp.zeros((tq, chunk), F32)
        for h in range(n_ih):
            sc = sc + jnp.maximum(d[h * tq:(h + 1) * tq], 0.0) * wcols[h]
        key = jnp.where(c * chunk + col_l <= row_t, _mono_key(sc), INT_MIN)
        for k in range(per_chunk):
            key_ref[c * per_chunk + k] = key[:, k * LANES:(k + 1) * LANES]
        return carry

    lax.fori_loop(0, n_chunks, chunk_body, 0)
    n_pieces = n_chunks * per_chunk

    def count(pred):
        def body(c, cnt):
            for k in range(per_chunk):
                cnt = cnt + jnp.where(pred(key_ref[c * per_chunk + k]), 1.0, 0.0)
            return cnt
        cnt = lax.fori_loop(0, n_chunks, body, jnp.zeros((tq, LANES), F32))
        return jnp.broadcast_to(jnp.sum(cnt, axis=1, keepdims=True), (tq, LANES))

    kf = float(k_top)
    zero = jnp.zeros((tq, LANES), jnp.int32)
    prefix = jnp.where(count(lambda k: k >= zero) >= kf, zero, INT_MIN)

    def bit_body(b, prefix):
        cand = prefix | jnp.left_shift(jnp.int32(1), 30 - b)
        return jnp.where(count(lambda k: k >= cand) >= kf, cand, prefix)

    tau = lax.fori_loop(0, 31, bit_body, prefix)
    n_ge = count(lambda k: k >= tau)
    tied = jnp.max(n_ge) > kf

    @pl.when(jnp.logical_not(tied))
    def _():
        def body(p, carry):
            key = key_ref[p]
            o_ref[0, 0, p] = jnp.where((key >= tau) & (key != INT_MIN), 0.0, MASK_NEG)
            return carry
        lax.fori_loop(0, n_pieces, body, 0)

    @pl.when(tied)
    def _():
        room = kf - count(lambda k: k > tau)
        before = (lax.broadcasted_iota(jnp.int32, (LANES, LANES), 0)
                  < lax.broadcasted_iota(jnp.int32, (LANES, LANES), 1)).astype(BF16)

        def body(p, seen):
            key = key_ref[p]
            tie = (key == tau) & (key != INT_MIN)
            tie_f = jnp.where(tie, 1.0, 0.0)
            rank = seen + jnp.dot(tie_f.astype(BF16), before, preferred_element_type=F32)
            o_ref[0, 0, p] = jnp.where((key > tau) | (tie & (rank < room)), 0.0, MASK_NEG)
            return seen + jnp.broadcast_to(jnp.sum(tie_f, axis=1, keepdims=True), (tq, LANES))
        lax.fori_loop(0, n_pieces, body, jnp.zeros((tq, LANES), F32))

    def fill(p, carry):
        o_ref[0, 0, p] = jnp.full((tq, LANES), MASK_NEG, F32)
        return carry
    lax.fori_loop(n_pieces, n_pieces_total, fill, 0)


def _indexer_mask(q_idx, k_idx, w_idx, k_top):
    B, n_ih, S, di = q_idx.shape
    tq = LANES
    chunk = min(S, 512)
    n_pieces = S // LANES
    kern = functools.partial(_indexer_kernel, k_top=k_top, chunk=chunk)
    return pl.pallas_call(
        kern,
        out_shape=jax.ShapeDtypeStruct((B, S // tq, n_pieces, tq, LANES), F32),
        grid=(B, S // tq),
        in_specs=[pl.BlockSpec((1, n_ih, tq, di), lambda b, i: (b, 0, i, 0)),
                  pl.BlockSpec((1, S, di), lambda b, i: (b, 0, 0)),
                  pl.BlockSpec((1, tq, n_ih), lambda b, i: (b, i, 0))],
        out_specs=pl.BlockSpec((1, 1, n_pieces, tq, LANES), lambda b, i: (b, i, 0, 0, 0)),
        scratch_shapes=[pltpu.VMEM((n_pieces, tq, LANES), jnp.int32)],
        compiler_params=_params(("parallel", "parallel")),
        name="indexer_mask",
    )(q_idx, k_idx, w_idx)


def _bias_tile_kernel(rb_ref, o_ref, *, thresholds):
    kind = pl.program_id(0)
    h = pl.program_id(1)
    nb = len(thresholds)
    d = (lax.broadcasted_iota(jnp.int32, (LANES, LANES), 0)
         - lax.broadcasted_iota(jnp.int32, (LANES, LANES), 1) + kind * LANES)
    d = jnp.maximum(d, 0)
    far = rb_ref[nb - 1, h]
    val = jnp.full((LANES, LANES), (rb_ref[0, h] - far) * LOG2E, F32)
    for b in range(1, nb):
        val = jnp.where(d >= thresholds[b], (rb_ref[b, h] - far) * LOG2E, val)
    o_ref[0, 0] = jnp.where(kind == 2, 0.0, val)


def _bias_tiles(rel_bias):
    nb, n_heads = rel_bias.shape
    thresholds = _t5_bucket_thresholds(nb)
    assert thresholds[nb - 1] <= LANES + 1, "bias must saturate within one 128-key block"
    return pl.pallas_call(
        functools.partial(_bias_tile_kernel, thresholds=thresholds),
        out_shape=jax.ShapeDtypeStruct((3, n_heads, LANES, LANES), F32),
        grid=(3, n_heads),
        in_specs=[pl.BlockSpec(memory_space=pltpu.SMEM)],
        out_specs=pl.BlockSpec((1, 1, LANES, LANES), lambda k, h: (k, h, 0, 0)),
        compiler_params=_params(("parallel", "parallel")),
        name="bias_tiles",
    )(rel_bias)


def _dsa_attn_kernel(qa_ref, ckv_ref, mk_ref, tab_ref, wuv_ref, o_ref, acc_ref, m_ref, l_ref,
                     *, scale_log2, tk, unroll):
    i = pl.program_id(1)
    n_heads, tq, c = qa_ref.shape[1:]
    dv = wuv_ref.shape[2]
    npk = tk // LANES

    acc_ref[...] = jnp.zeros_like(acc_ref)
    m_ref[...] = jnp.full_like(m_ref, M_INIT)
    l_ref[...] = jnp.zeros_like(l_ref)

    def tile_kind(dist):
        return jnp.where(dist == 0, 0, jnp.where(dist == 1, 1, 2))

    def lanes(v, n):
        return jnp.concatenate([v] * (n // LANES), axis=1)

    def kv_body(j, carry):
        k0 = pl.multiple_of(j * tk, tk)
        p0 = npk * j

        def head_group(g, with_bias):
            h0 = pl.multiple_of(g * unroll, unroll)
            qa_g, acc_g = qa_ref.at[0, pl.ds(h0, unroll)], acc_ref.at[pl.ds(h0, unroll)]
            m_g, l_g = m_ref.at[pl.ds(h0, unroll)], l_ref.at[pl.ds(h0, unroll)]
            kv = ckv_ref[0, pl.ds(k0, tk), :]
            madd = jnp.concatenate([mk_ref[0, 0, p0 + k] for k in range(npk)], axis=1)
            half = unroll // 2
            s_halves = [lax.dot_general(qa_g[pl.ds(a * half, half)].reshape(half * tq, c), kv, NT_DIMS,
                                        preferred_element_type=F32) for a in range(2)]
            ps, alphas = [], []
            for u in range(unroll):
                s = s_halves[u // half][(u % half) * tq:(u % half + 1) * tq] * scale_log2 + madd
                if with_bias:
                    s = s + jnp.concatenate(
                        [tab_ref.at[tile_kind(i - (p0 + k)), pl.ds(h0, unroll)][u] for k in range(npk)], axis=1)
                m_old = m_g[u]
                m_new = jnp.maximum(m_old, jnp.max(s, axis=1, keepdims=True))
                alpha = jnp.exp2(m_old - m_new)
                p = jnp.exp2(s - lanes(m_new, tk))
                l_g[u] = alpha * l_g[u] + jnp.sum(p, axis=1, keepdims=True)
                m_g[u] = m_new
                ps.append(p.astype(BF16))
                alphas.append(alpha)
            pv_halves = [jnp.dot(jnp.concatenate(ps[a * half:(a + 1) * half], axis=0), kv,
                                 preferred_element_type=F32) for a in range(2)]
            for u in range(unroll):
                acc_g[u] = (lanes(alphas[u], c) * acc_g[u]
                            + pv_halves[u // half][(u % half) * tq:(u % half + 1) * tq])

        near = (i - p0) <= npk

        @pl.when(near)
        def _():
            lax.fori_loop(0, n_heads // unroll, lambda g, cr: (head_group(g, True), cr)[1], 0)

        @pl.when(jnp.logical_not(near))
        def _():
            lax.fori_loop(0, n_heads // unroll, lambda g, cr: (head_group(g, False), cr)[1], 0)

        return carry

    n_kb = (i * tq + tq - 1) // tk + 1
    lax.fori_loop(0, n_kb, kv_body, 0)

    for h in range(n_heads):
        o_lat = (acc_ref[h] / lanes(l_ref[h], c)).astype(BF16)
        o_ref[0, :, h * dv:(h + 1) * dv] = jnp.dot(o_lat, wuv_ref[h], preferred_element_type=F32).astype(BF16)


def _dsa_attention(q_abs, c_kv, mask, bias_tab, w_uv, scale):
    B, n_heads, S, c = q_abs.shape
    dv = w_uv.shape[2]
    tq = LANES
    tk = min(S, 2 * LANES)
    n_pieces = S // LANES
    kern = functools.partial(_dsa_attn_kernel, scale_log2=scale * LOG2E, tk=tk, unroll=min(n_heads, 8))
    return pl.pallas_call(
        kern,
        out_shape=jax.ShapeDtypeStruct((B, S, n_heads * dv), BF16),
        grid=(B, S // tq),
        in_specs=[pl.BlockSpec((1, n_heads, tq, c), lambda b, i: (b, 0, i, 0)),
                  pl.BlockSpec((1, S, c), lambda b, i: (b, 0, 0)),
                  pl.BlockSpec((1, 1, n_pieces, tq, LANES), lambda b, i: (b, i, 0, 0, 0)),
                  pl.BlockSpec((3, n_heads, LANES, LANES), lambda b, i: (0, 0, 0, 0)),
                  pl.BlockSpec((n_heads, c, dv), lambda b, i: (0, 0, 0))],
        out_specs=pl.BlockSpec((1, tq, n_heads * dv), lambda b, i: (b, i, 0)),
        scratch_shapes=[pltpu.VMEM((n_heads, tq, c), F32),
                        pltpu.VMEM((n_heads, tq, LANES), F32),
                        pltpu.VMEM((n_heads, tq, LANES), F32)],
        compiler_params=_params(("parallel", "parallel")),
        name="dsa_attention",
    )(q_abs, c_kv, mask, bias_tab, w_uv)


def _sb_attn_kernel(q_ref, k_ref, v_ref, o_ref, acc_ref, rest_ref, *, scale, dh):
    i = pl.program_id(2)
    tq = q_ref.shape[1]
    tk = tq
    nh = q_ref.shape[2] // dh
    below = (lax.broadcasted_iota(jnp.int32, (tq, tk), 1)
             < lax.broadcasted_iota(jnp.int32, (tq, tk), 0))
    after = (lax.broadcasted_iota(jnp.int32, (2 * tk, tk), 0) % tk
             > lax.broadcasted_iota(jnp.int32, (2 * tk, tk), 1)).astype(BF16)

    acc_ref[...] = jnp.zeros_like(acc_ref)
    rest_ref[...] = jnp.zeros_like(rest_ref)

    def block(j, diag):
        k0 = pl.multiple_of(j * tk, tk)
        log_betas, log_rests, parts = [], [], []
        for u in range(nh):
            hd = slice(u * dh, (u + 1) * dh)
            z2 = lax.dot_general(q_ref[0, :, hd], k_ref[0, pl.ds(k0, tk), hd], NT_DIMS,
                                 preferred_element_type=F32) * (scale * LOG2E)
            neg_abs = lax.bitcast_convert_type(
                lax.bitcast_convert_type(z2, jnp.uint32) | jnp.uint32(0x80000000), F32)
            log_beta = jnp.minimum(z2, 0.0) - jnp.log2(1.0 + jnp.exp2(neg_abs))
            log_rest = log_beta - z2
            if diag:
                log_rest = jnp.where(below, log_rest, 0.0)
            hi = lax.bitcast_convert_type(
                lax.bitcast_convert_type(log_rest, jnp.uint32) & jnp.uint32(0xFFFF0000), F32)
            parts.append(jnp.concatenate([hi.astype(BF16), (log_rest - hi).astype(BF16)], axis=1))
            log_betas.append(log_beta)
            log_rests.append(log_rest)
        cum = jnp.dot(jnp.concatenate(parts, axis=0), after, preferred_element_type=F32)
        for u in range(nh):
            hd = slice(u * dh, (u + 1) * dh)
            rest = rest_ref[u]
            later = cum[u * tq:(u + 1) * tq] + jnp.concatenate([rest] * (tk // LANES), axis=1)
            wgt = jnp.exp2(log_betas[u] + later)
            if diag:
                wgt = jnp.where(below, wgt, 0.0)
            acc_ref[:, hd] += jnp.dot(wgt.astype(BF16), v_ref[0, pl.ds(k0, tk), hd], preferred_element_type=F32)
            rest_ref[u] = rest + jnp.sum(log_rests[u], axis=1, keepdims=True)

    block(i, True)
    lax.fori_loop(1, i + 1, lambda jj, cr: (block(i - jj, False), cr)[1], 0)
    o_ref[0] = acc_ref[...].astype(o_ref.dtype)


def _sb_attention(qkv, n_heads):
    B, S, three_d = qkv.shape
    dh = three_d // (3 * n_heads)
    tq = min(S, 2 * LANES)
    nh = min(n_heads, 4)
    n_groups = n_heads // nh
    kern = functools.partial(_sb_attn_kernel, scale=float(dh ** -0.5), dh=dh)
    return pl.pallas_call(
        kern,
        out_shape=jax.ShapeDtypeStruct((B, S, n_heads * dh), BF16),
        grid=(B, n_groups, S // tq),
        in_specs=[pl.BlockSpec((1, tq, nh * dh), lambda b, g, i: (b, i, g)),
                  pl.BlockSpec((1, S, nh * dh), lambda b, g, i: (b, 0, n_groups + g)),
                  pl.BlockSpec((1, S, nh * dh), lambda b, g, i: (b, 0, 2 * n_groups + g))],
        out_specs=pl.BlockSpec((1, tq, nh * dh), lambda b, g, i: (b, i, g)),
        scratch_shapes=[pltpu.VMEM((tq, nh * dh), F32), pltpu.VMEM((nh, tq, LANES), F32)],
        compiler_params=_params(("parallel", "parallel", "parallel")),
        name="sb_attention",
    )(qkv, qkv, qkv)


def kernel(x, c, rel_bias, ada_w, ada_b, norm_g, a_w_in, a_q_norm, a_kv_norm, a_idx_k_norm, a_w_qb, a_w_idx_qb,
           a_w_uk, a_w_uv, a_w_out, b_w_in, b_w_out, mlp_w1, mlp_w2):
    B, S, D = x.shape
    depth = ada_w.shape[0]
    n_mixers = 2

    mod = _ada_mod(c, ada_w, ada_b)
    modr = mod.reshape(depth * B * 6, 1, D)

    def mod_row(layer, chunk):
        return lambda b: (layer * B + b) * 6 + chunk

    rq, rkv, di = a_q_norm.shape[1], a_kv_norm.shape[1], a_idx_k_norm.shape[1]
    n_a_heads, _, a_qk_dim = a_w_uk.shape[1:]
    n_idx_heads = a_w_idx_qb.shape[2] // di
    k_top = min(IDX_TOPK, S // 4)
    bias_tab = _bias_tiles(rel_bias)

    for layer in range(depth):
        j = layer // n_mixers
        g = norm_g[layer]
        if layer % n_mixers == 0:
            w_in = a_w_in[j].astype(BF16)
            wwi = jnp.pad(w_in[:, rq + rkv + di:], ((0, 0), (0, LANES - n_idx_heads)))
            q_abs, q_idx, c_kv, k_idx, w_idx = _dsa_proj(
                x, g[0], modr, mod_row(layer, 0), mod_row(layer, 1),
                w_in[:, :rq], w_in[:, rq:rq + rkv], w_in[:, rq + rkv:rq + rkv + di], wwi,
                a_q_norm[j], a_kv_norm[j], a_idx_k_norm[j],
                a_w_qb[j].astype(BF16), a_w_idx_qb[j].astype(BF16),
                jnp.swapaxes(a_w_uk[j], 1, 2).astype(BF16), n_idx_heads)
            mask = _indexer_mask(q_idx, k_idx, w_idx, k_top)
            y = _dsa_attention(q_abs, c_kv, mask, bias_tab, a_w_uv[j].astype(BF16), float(a_qk_dim ** -0.5))
            w_out = a_w_out[j].astype(BF16)
        else:
            qkv = _in_proj(x, g[0], modr, mod_row(layer, 0), mod_row(layer, 1), b_w_in[j].astype(BF16), BF16)
            y = _sb_attention(qkv, B_HEADS)
            w_out = b_w_out[j].astype(BF16)
        x = _out_proj(y, w_out, x, g[1], modr, mod_row(layer, 2))
        x = _mlp(x, g[2], g[3], modr, mod_row(layer, 3), mod_row(layer, 4), mod_row(layer, 5),
                 mlp_w1[layer].astype(BF16), mlp_w2[layer].astype(BF16))
    return x
```

```python
import functools
import math

import numpy as np
import jax
import jax.numpy as jnp
from jax import lax
from jax.experimental import pallas as pl
from jax.experimental.pallas import tpu as pltpu

F32 = jnp.float32
BF16 = jnp.bfloat16

RMS_EPS = 1e-6
B_HEADS = 16
IDX_TOPK = 256
REL_MAX_DIST = 128

LANES = 128
V7X_VMEM_LIMIT_BYTES = 56 * 1024 * 1024

MASK_NEG = -1e30
M_INIT = -1e29
INT_MIN = int(np.iinfo(np.int32).min)
LOG2E = math.log2(math.e)
SB_UNDERFLOW_LOG2 = 160.0

NT_DIMS = (((1,), (1,)), ((), ()))


def _params(semantics):
    return pltpu.CompilerParams(dimension_semantics=semantics,
                                vmem_limit_bytes=V7X_VMEM_LIMIT_BYTES)


def _rms(x, g):
    return x * lax.rsqrt(jnp.mean(x * x, axis=-1, keepdims=True) + RMS_EPS) * g


def _normmod(x, g, shift, scale):
    return _rms(x, g) * (1.0 + scale) + shift


def _t5_bucket_thresholds(num_buckets):
    max_exact = num_buckets // 2

    def bucket(d):
        if d < max_exact:
            return d
        large = max_exact + int(math.log(max(d, 1) / max_exact)
                                / math.log(REL_MAX_DIST / max_exact) * (num_buckets - max_exact))
        return min(large, num_buckets - 1)

    thr, d = [], 0
    for b in range(num_buckets):
        while bucket(d) < b:
            d += 1
        thr.append(d)
    return tuple(thr)


def _ada_kernel(c_ref, w_ref, b_ref, o_ref):
    c = c_ref[...]
    s = c * jax.nn.sigmoid(c)
    o_ref[0] = jnp.dot(s.astype(BF16), w_ref[0].astype(BF16), preferred_element_type=F32) + b_ref[0]


def _ada_mod(c, ada_w, ada_b):
    L, D, E = ada_w.shape
    B = c.shape[0]
    tn = min(E, 1024)
    return pl.pallas_call(
        _ada_kernel,
        out_shape=jax.ShapeDtypeStruct((L, B, E), F32),
        grid=(L, E // tn),
        in_specs=[pl.BlockSpec((B, D), lambda l, j: (0, 0)),
                  pl.BlockSpec((1, D, tn), lambda l, j: (l, 0, j)),
                  pl.BlockSpec((1, 1, tn), lambda l, j: (l, 0, j))],
        out_specs=pl.BlockSpec((1, B, tn), lambda l, j: (l, 0, j)),
        compiler_params=_params(("parallel", "parallel")),
        name="ada_mod",
    )(c, ada_w, ada_b.reshape(L, 1, E))


def _in_proj_kernel(x_ref, g_ref, sh_ref, sc_ref, w_ref, o_ref, h_ref):
    @pl.when(pl.program_id(2) == 0)
    def _():
        h_ref[...] = _normmod(x_ref[0], g_ref[...], sh_ref[0], sc_ref[0]).astype(BF16)

    o_ref[0] = jnp.dot(h_ref[...], w_ref[...], preferred_element_type=F32).astype(o_ref.dtype)


def _in_proj(x, g, modr, shift_row, scale_row, w, out_dtype):
    B, S, D = x.shape
    n_out = w.shape[1]
    tm = min(S, 512)
    tn = min(n_out, 1024)
    return pl.pallas_call(
        _in_proj_kernel,
        out_shape=jax.ShapeDtypeStruct((B, S, n_out), out_dtype),
        grid=(B, S // tm, n_out // tn),
        in_specs=[pl.BlockSpec((1, tm, D), lambda b, i, j: (b, i, 0)),
                  pl.BlockSpec((1, D), lambda b, i, j: (0, 0)),
                  pl.BlockSpec((1, 1, D), lambda b, i, j: (shift_row(b), 0, 0)),
                  pl.BlockSpec((1, 1, D), lambda b, i, j: (scale_row(b), 0, 0)),
                  pl.BlockSpec((D, tn), lambda b, i, j: (0, j))],
        out_specs=pl.BlockSpec((1, tm, tn), lambda b, i, j: (b, i, j)),
        scratch_shapes=[pltpu.VMEM((tm, D), BF16)],
        compiler_params=_params(("parallel", "parallel", "arbitrary")),
        name="in_proj",
    )(x, g.reshape(1, D), modr, modr, w)


def _out_proj_kernel(a_ref, w_ref, x_ref, g_ref, gate_ref, o_ref, acc_ref):
    k = pl.program_id(2)

    @pl.when(k == 0)
    def _():
        acc_ref[...] = jnp.zeros_like(acc_ref)

    acc_ref[...] += jnp.dot(a_ref[0], w_ref[...], preferred_element_type=F32)

    @pl.when(k == pl.num_programs(2) - 1)
    def _():
        o_ref[0] = x_ref[0] + gate_ref[0] * _rms(acc_ref[...], g_ref[...])


def _out_proj(a, w, x, g, modr, gate_row):
    B, S, D = x.shape
    kin = a.shape[2]
    tm = min(S, 512)
    tk = min(kin, 2048)
    return pl.pallas_call(
        _out_proj_kernel,
        out_shape=jax.ShapeDtypeStruct((B, S, D), F32),
        grid=(B, S // tm, kin // tk),
        in_specs=[pl.BlockSpec((1, tm, tk), lambda b, i, k: (b, i, k)),
                  pl.BlockSpec((tk, D), lambda b, i, k: (k, 0)),
                  pl.BlockSpec((1, tm, D), lambda b, i, k: (b, i, 0)),
                  pl.BlockSpec((1, D), lambda b, i, k: (0, 0)),
                  pl.BlockSpec((1, 1, D), lambda b, i, k: (gate_row(b), 0, 0))],
        out_specs=pl.BlockSpec((1, tm, D), lambda b, i, k: (b, i, 0)),
        scratch_shapes=[pltpu.VMEM((tm, D), F32)],
        compiler_params=_params(("parallel", "parallel", "arbitrary")),
        name="out_proj",
    )(a, w, x, g.reshape(1, D), modr)


def _mlp_kernel(x_ref, g_in_ref, sh_ref, sc_ref, w1_ref, w2_ref, g_out_ref, gate_ref, o_ref, h_ref, acc_ref):
    f = pl.program_id(2)

    @pl.when(f == 0)
    def _():
        h_ref[...] = _normmod(x_ref[0], g_in_ref[...], sh_ref[0], sc_ref[0]).astype(BF16)
        acc_ref[...] = jnp.zeros_like(acc_ref)

    a = jnp.dot(h_ref[...], w1_ref[...], preferred_element_type=F32)
    a = jnp.square(jnp.maximum(a, 0.0)).astype(BF16)
    acc_ref[...] += jnp.dot(a, w2_ref[...], preferred_element_type=F32)

    @pl.when(f == pl.num_programs(2) - 1)
    def _():
        o_ref[0] = x_ref[0] + gate_ref[0] * _rms(acc_ref[...], g_out_ref[...])


def _mlp(x, g_in, g_out, modr, shift_row, scale_row, gate_row, w1, w2):
    B, S, D = x.shape
    dff = w1.shape[1]
    tm = min(S, 512)
    tf = min(dff, 512)
    return pl.pallas_call(
        _mlp_kernel,
        out_shape=jax.ShapeDtypeStruct((B, S, D), F32),
        grid=(B, S // tm, dff // tf),
        in_specs=[pl.BlockSpec((1, tm, D), lambda b, i, f: (b, i, 0)),
                  pl.BlockSpec((1, D), lambda b, i, f: (0, 0)),
                  pl.BlockSpec((1, 1, D), lambda b, i, f: (shift_row(b), 0, 0)),
                  pl.BlockSpec((1, 1, D), lambda b, i, f: (scale_row(b), 0, 0)),
                  pl.BlockSpec((D, tf), lambda b, i, f: (0, f)),
                  pl.BlockSpec((tf, D), lambda b, i, f: (f, 0)),
                  pl.BlockSpec((1, D), lambda b, i, f: (0, 0)),
                  pl.BlockSpec((1, 1, D), lambda b, i, f: (gate_row(b), 0, 0))],
        out_specs=pl.BlockSpec((1, tm, D), lambda b, i, f: (b, i, 0)),
        scratch_shapes=[pltpu.VMEM((tm, D), BF16), pltpu.VMEM((tm, D), F32)],
        compiler_params=_params(("parallel", "parallel", "arbitrary")),
        name="mlp",
    )(x, g_in.reshape(1, D), modr, modr, w1, w2, g_out.reshape(1, D), modr)


def _dsa_proj_kernel(x_ref, g_ref, sh_ref, sc_ref, wq_ref, wkv_ref, wki_ref, wwi_ref,
                     qn_ref, kvn_ref, kin_ref, wqb_ref, wiqb_ref, wuk_ref,
                     qa_ref, qi_ref, ckv_ref, ki_ref, wi_ref, *, w_idx_scale):
    h = _normmod(x_ref[0], g_ref[...], sh_ref[0], sc_ref[0]).astype(BF16)
    q_lat = _rms(jnp.dot(h, wq_ref[...], preferred_element_type=F32), qn_ref[...]).astype(BF16)
    ckv_ref[0] = _rms(jnp.dot(h, wkv_ref[...], preferred_element_type=F32), kvn_ref[...]).astype(BF16)

    ki = jnp.dot(h, wki_ref[...], preferred_element_type=F32)
    mu = jnp.mean(ki, axis=-1, keepdims=True)
    var = jnp.mean(jnp.square(ki - mu), axis=-1, keepdims=True)
    ki_ref[0] = ((ki - mu) * lax.rsqrt(var + RMS_EPS) * kin_ref[...]).astype(BF16)

    n_ih = wi_ref.shape[2]
    wi = jnp.dot(h, wwi_ref[...], preferred_element_type=F32)
    wi_ref[0] = wi[:, :n_ih] * w_idx_scale

    n_heads, dh = wuk_ref.shape[0], wuk_ref.shape[1]
    q = jnp.dot(q_lat, wqb_ref[...], preferred_element_type=F32)
    for hd in range(n_heads):
        qh = q[:, hd * dh:(hd + 1) * dh].astype(BF16)
        qa_ref[0, hd] = jnp.dot(qh, wuk_ref[hd], preferred_element_type=F32).astype(BF16)

    di = qi_ref.shape[3]
    qidx = jnp.dot(q_lat, wiqb_ref[...], preferred_element_type=F32)
    for hd in range(qi_ref.shape[1]):
        qi_ref[0, hd] = qidx[:, hd * di:(hd + 1) * di].astype(BF16)


def _dsa_proj(x, g, modr, shift_row, scale_row, wq, wkv, wki, wwi, q_norm, kv_norm, idx_k_norm,
              w_qb, w_idx_qb, w_uk_t, n_idx_heads):
    B, S, D = x.shape
    rq, rkv, di = wq.shape[1], wkv.shape[1], wki.shape[1]
    n_heads, dh, _ = w_uk_t.shape
    tm = min(S, 256)
    const = lambda *shape: pl.BlockSpec(shape, lambda b, i: (0,) * len(shape))
    kern = functools.partial(_dsa_proj_kernel, w_idx_scale=float(n_idx_heads ** -0.5 * di ** -0.5))
    return pl.pallas_call(
        kern,
        out_shape=(jax.ShapeDtypeStruct((B, n_heads, S, rkv), BF16),
                   jax.ShapeDtypeStruct((B, n_idx_heads, S, di), BF16),
                   jax.ShapeDtypeStruct((B, S, rkv), BF16),
                   jax.ShapeDtypeStruct((B, S, di), BF16),
                   jax.ShapeDtypeStruct((B, S, n_idx_heads), F32)),
        grid=(B, S // tm),
        in_specs=[pl.BlockSpec((1, tm, D), lambda b, i: (b, i, 0)),
                  const(1, D),
                  pl.BlockSpec((1, 1, D), lambda b, i: (shift_row(b), 0, 0)),
                  pl.BlockSpec((1, 1, D), lambda b, i: (scale_row(b), 0, 0)),
                  const(D, rq), const(D, rkv), const(D, di), const(D, LANES),
                  const(1, rq), const(1, rkv), const(1, di),
                  const(rq, n_heads * dh), const(rq, n_idx_heads * di), const(n_heads, dh, rkv)],
        out_specs=(pl.BlockSpec((1, n_heads, tm, rkv), lambda b, i: (b, 0, i, 0)),
                   pl.BlockSpec((1, n_idx_heads, tm, di), lambda b, i: (b, 0, i, 0)),
                   pl.BlockSpec((1, tm, rkv), lambda b, i: (b, i, 0)),
                   pl.BlockSpec((1, tm, di), lambda b, i: (b, i, 0)),
                   pl.BlockSpec((1, tm, n_idx_heads), lambda b, i: (b, i, 0))),
        compiler_params=_params(("parallel", "parallel")),
        name="dsa_proj",
    )(x, g.reshape(1, D), modr, modr, wq, wkv, wki, wwi,
      q_norm.reshape(1, rq), kv_norm.reshape(1, rkv), idx_k_norm.reshape(1, di),
      w_qb, w_idx_qb, w_uk_t)


def _mono_key(score):
    score = jnp.where(score == 0.0, 0.0, score)
    bits = lax.bitcast_convert_type(score, jnp.int32)
    return bits ^ ((bits >> 31) & 0x7FFFFFFF)


def _indexer_kernel(qi_ref, ki_ref, w_ref, o_ref, key_ref, *, k_top, chunk):
    i = pl.program_id(1)
    n_ih, tq, di = qi_ref.shape[1:]
    n_pieces_total = o_ref.shape[2]
    per_chunk = chunk // LANES

    q2 = qi_ref[0].reshape(n_ih * tq, di)
    w = w_ref[0]
    wcols = [w[:, h:h + 1] for h in range(n_ih)]
    n_chunks = (i * tq + tq - 1) // chunk + 1
    row_t = i * tq + lax.broadcasted_iota(jnp.int32, (tq, chunk), 0)
    col_l = lax.broadcasted_iota(jnp.int32, (tq, chunk), 1)

    def chunk_body(c, carry):
        kc = ki_ref[0, pl.ds(pl.multiple_of(c * chunk, chunk), chunk), :]
        d = lax.dot_general(q2, kc, NT_DIMS, preferred_element_type=F32)
        sc = jnp.zeros((tq, chunk), F32)
        for h in range(n_ih):
            sc = sc + jnp.maximum(d[h * tq:(h + 1) * tq], 0.0) * wcols[h]
        key = jnp.where(c * chunk + col_l <= row_t, _mono_key(sc), INT_MIN)
        for k in range(per_chunk):
            key_ref[c * per_chunk + k] = key[:, k * LANES:(k + 1) * LANES]
        return carry

    lax.fori_loop(0, n_chunks, chunk_body, 0)
    n_pieces = n_chunks * per_chunk

    def count(pred):
        def body(c, cnt):
            for k in range(per_chunk):
                cnt = cnt + jnp.where(pred(key_ref[c * per_chunk + k]), 1.0, 0.0)
            return cnt
        cnt = lax.fori_loop(0, n_chunks, body, jnp.zeros((tq, LANES), F32))
        return jnp.broadcast_to(jnp.sum(cnt, axis=1, keepdims=True), (tq, LANES))

    kf = float(k_top)
    zero = jnp.zeros((tq, LANES), jnp.int32)
    prefix = jnp.where(count(lambda k: k >= zero) >= kf, zero, INT_MIN)

    def bit_body(b, prefix):
        cand = prefix | jnp.left_shift(jnp.int32(1), 30 - b)
        return jnp.where(count(lambda k: k >= cand) >= kf, cand, prefix)

    tau = lax.fori_loop(0, 31, bit_body, prefix)
    n_ge = count(lambda k: k >= tau)
    tied = jnp.max(n_ge) > kf

    @pl.when(jnp.logical_not(tied))
    def _():
        def body(p, carry):
            key = key_ref[p]
            o_ref[0, 0, p] = jnp.where((key >= tau) & (key != INT_MIN), 0.0, MASK_NEG)
            return carry
        lax.fori_loop(0, n_pieces, body, 0)

    @pl.when(tied)
    def _():
        room = kf - count(lambda k: k > tau)
        before = (lax.broadcasted_iota(jnp.int32, (LANES, LANES), 0)
                  < lax.broadcasted_iota(jnp.int32, (LANES, LANES), 1)).astype(BF16)

        def body(p, seen):
            key = key_ref[p]
            tie = (key == tau) & (key != INT_MIN)
            tie_f = jnp.where(tie, 1.0, 0.0)
            rank = seen + jnp.dot(tie_f.astype(BF16), before, preferred_element_type=F32)
            o_ref[0, 0, p] = jnp.where((key > tau) | (tie & (rank < room)), 0.0, MASK_NEG)
            return seen + jnp.broadcast_to(jnp.sum(tie_f, axis=1, keepdims=True), (tq, LANES))
        lax.fori_loop(0, n_pieces, body, jnp.zeros((tq, LANES), F32))

    def fill(p, carry):
        o_ref[0, 0, p] = jnp.full((tq, LANES), MASK_NEG, F32)
        return carry
    lax.fori_loop(n_pieces, n_pieces_total, fill, 0)


def _indexer_mask(q_idx, k_idx, w_idx, k_top):
    B, n_ih, S, di = q_idx.shape
    tq = LANES
    chunk = min(S, 512)
    n_pieces = S // LANES
    kern = functools.partial(_indexer_kernel, k_top=k_top, chunk=chunk)
    return pl.pallas_call(
        kern,
        out_shape=jax.ShapeDtypeStruct((B, S // tq, n_pieces, tq, LANES), F32),
        grid=(B, S // tq),
        in_specs=[pl.BlockSpec((1, n_ih, tq, di), lambda b, i: (b, 0, i, 0)),
                  pl.BlockSpec((1, S, di), lambda b, i: (b, 0, 0)),
                  pl.BlockSpec((1, tq, n_ih), lambda b, i: (b, i, 0))],
        out_specs=pl.BlockSpec((1, 1, n_pieces, tq, LANES), lambda b, i: (b, i, 0, 0, 0)),
        scratch_shapes=[pltpu.VMEM((n_pieces, tq, LANES), jnp.int32)],
        compiler_params=_params(("parallel", "parallel")),
        name="indexer_mask",
    )(q_idx, k_idx, w_idx)


def _bias_tile_kernel(rb_ref, o_ref, *, thresholds):
    kind = pl.program_id(0)
    h = pl.program_id(1)
    nb = len(thresholds)
    d = (lax.broadcasted_iota(jnp.int32, (LANES, LANES), 0)
         - lax.broadcasted_iota(jnp.int32, (LANES, LANES), 1) + kind * LANES)
    d = jnp.maximum(d, 0)
    far = rb_ref[nb - 1, h]
    val = jnp.full((LANES, LANES), (rb_ref[0, h] - far) * LOG2E, F32)
    for b in range(1, nb):
        val = jnp.where(d >= thresholds[b], (rb_ref[b, h] - far) * LOG2E, val)
    o_ref[0, 0] = jnp.where(kind == 2, 0.0, val)


def _bias_tiles(rel_bias):
    nb, n_heads = rel_bias.shape
    thresholds = _t5_bucket_thresholds(nb)
    assert thresholds[nb - 1] <= LANES + 1, "bias must saturate within one 128-key block"
    return pl.pallas_call(
        functools.partial(_bias_tile_kernel, thresholds=thresholds),
        out_shape=jax.ShapeDtypeStruct((3, n_heads, LANES, LANES), F32),
        grid=(3, n_heads),
        in_specs=[pl.BlockSpec(memory_space=pltpu.SMEM)],
        out_specs=pl.BlockSpec((1, 1, LANES, LANES), lambda k, h: (k, h, 0, 0)),
        compiler_params=_params(("parallel", "parallel")),
        name="bias_tiles",
    )(rel_bias)


def _dsa_attn_kernel(qa_ref, ckv_ref, mk_ref, tab_ref, wuv_ref, o_ref,
                     acc_ref, m_ref, l_ref, s0_ref, s1_ref, mb0_ref, mb1_ref, p_ref, a_ref,
                     *, scale_log2, tk, unroll):
    i = pl.program_id(1)
    n_heads, tq, c = qa_ref.shape[1:]
    dv = wuv_ref.shape[2]
    npk = tk // LANES

    acc_ref[...] = jnp.zeros_like(acc_ref)
    m_ref[...] = jnp.full_like(m_ref, M_INIT)
    l_ref[...] = jnp.zeros_like(l_ref)

    def tile_kind(dist):
        return jnp.where(dist == 0, 0, jnp.where(dist == 1, 1, 2))

    def lanes(v, n):
        return jnp.concatenate([v] * (n // LANES), axis=1)

    n_groups = n_heads // unroll
    half = unroll // 2

    def split(n):
        return n // n_groups, pl.multiple_of((n % n_groups) * unroll, unroll)

    def scores(n, s_ref, mb_ref, with_bias):
        j, h0 = split(n)
        p0 = npk * j
        kv = ckv_ref[0, pl.ds(pl.multiple_of(j * tk, tk), tk), :]
        madd = jnp.concatenate([mk_ref[0, 0, p0 + k] for k in range(npk)], axis=1)
        for a in range(2):
            q = qa_ref[0, pl.ds(h0 + a * half, half)].reshape(half * tq, c)
            sa = lax.dot_general(q, kv, NT_DIMS, preferred_element_type=F32)
            for u in range(a * half, (a + 1) * half):
                s = sa[(u - a * half) * tq:(u - a * half + 1) * tq] * scale_log2 + madd
                if with_bias:
                    s = s + jnp.concatenate(
                        [tab_ref.at[tile_kind(i - (p0 + k)), pl.ds(h0, unroll)][u] for k in range(npk)], axis=1)
                s_ref[u * tq:(u + 1) * tq, :] = s
                mb_ref[u] = jnp.broadcast_to(jnp.max(s, axis=1, keepdims=True), (tq, LANES))

    def reduce(n, s_ref, mb_ref):
        j, h0 = split(n)
        acc_g = acc_ref.at[pl.ds(h0, unroll)]
        m_g, l_g = m_ref.at[pl.ds(h0, unroll)], l_ref.at[pl.ds(h0, unroll)]
        kv = ckv_ref[0, pl.ds(pl.multiple_of(j * tk, tk), tk), :]
        for u in range(unroll):
            m_old = m_g[u]
            m_new = jnp.maximum(m_old, mb_ref[u])
            alpha = jnp.exp2(m_old - m_new)
            p = jnp.exp2(s_ref[u * tq:(u + 1) * tq, :] - lanes(m_new, tk))
            l_g[u] = alpha * l_g[u] + jnp.sum(p, axis=1, keepdims=True)
            m_g[u] = m_new
            p_ref[u * tq:(u + 1) * tq, :] = p.astype(BF16)
            a_ref[u] = alpha
        for a in range(2):
            pv = jnp.dot(p_ref[a * half * tq:(a + 1) * half * tq, :], kv, preferred_element_type=F32)
            for u in range(a * half, (a + 1) * half):
                acc_g[u] = lanes(a_ref[u], c) * acc_g[u] + pv[(u - a * half) * tq:(u - a * half + 1) * tq]

    def run(n_lo, n_hi, with_bias):
        @pl.when(n_hi > n_lo)
        def _():
            scores(n_lo, s0_ref, mb0_ref, with_bias)

            def pair(k, carry):
                n = n_lo + 2 * k
                scores(n + 1, s1_ref, mb1_ref, with_bias)
                reduce(n, s0_ref, mb0_ref)
                scores(jnp.minimum(n + 2, n_hi - 1), s0_ref, mb0_ref, with_bias)
                reduce(n + 1, s1_ref, mb1_ref)
                return carry

            lax.fori_loop(0, (n_hi - n_lo) // 2, pair, 0)

    n_kb = (i * tq + tq - 1) // tk + 1
    first_near = jnp.maximum((i - npk + 1) // npk, 0)
    run(0, first_near * n_groups, False)
    run(first_near * n_groups, n_kb * n_groups, True)

    for h in range(n_heads):
        o_lat = (acc_ref[h] * lanes(1.0 / l_ref[h], c)).astype(BF16)
        o_ref[0, :, h * dv:(h + 1) * dv] = jnp.dot(o_lat, wuv_ref[h], preferred_element_type=F32).astype(BF16)


def _dsa_attention(q_abs, c_kv, mask, bias_tab, w_uv, scale):
    B, n_heads, S, c = q_abs.shape
    dv = w_uv.shape[2]
    tq = LANES
    tk = min(S, 2 * LANES)
    n_pieces = S // LANES
    unroll = min(n_heads, 8)
    assert (n_heads // unroll) % 2 == 0 and unroll % 2 == 0
    kern = functools.partial(_dsa_attn_kernel, scale_log2=scale * LOG2E, tk=tk, unroll=unroll)
    return pl.pallas_call(
        kern,
        out_shape=jax.ShapeDtypeStruct((B, S, n_heads * dv), BF16),
        grid=(B, S // tq),
        in_specs=[pl.BlockSpec((1, n_heads, tq, c), lambda b, i: (b, 0, i, 0)),
                  pl.BlockSpec((1, S, c), lambda b, i: (b, 0, 0)),
                  pl.BlockSpec((1, 1, n_pieces, tq, LANES), lambda b, i: (b, i, 0, 0, 0)),
                  pl.BlockSpec((3, n_heads, LANES, LANES), lambda b, i: (0, 0, 0, 0)),
                  pl.BlockSpec((n_heads, c, dv), lambda b, i: (0, 0, 0))],
        out_specs=pl.BlockSpec((1, tq, n_heads * dv), lambda b, i: (b, i, 0)),
        scratch_shapes=[pltpu.VMEM((n_heads, tq, c), F32),
                        pltpu.VMEM((n_heads, tq, LANES), F32),
                        pltpu.VMEM((n_heads, tq, LANES), F32),
                        pltpu.VMEM((unroll * tq, tk), F32),
                        pltpu.VMEM((unroll * tq, tk), F32),
                        pltpu.VMEM((unroll, tq, LANES), F32),
                        pltpu.VMEM((unroll, tq, LANES), F32),
                        pltpu.VMEM((unroll * tq, tk), BF16),
                        pltpu.VMEM((unroll, tq, LANES), F32)],
        compiler_params=_params(("parallel", "parallel")),
        name="dsa_attention",
    )(q_abs, c_kv, mask, bias_tab, w_uv)


def _sb_attn_kernel(q_ref, k_ref, v_ref, o_ref, acc_ref, rest_ref, *, scale, dh):
    i = pl.program_id(2)
    tq = q_ref.shape[1]
    tk = tq
    nh = q_ref.shape[2] // dh
    below = (lax.broadcasted_iota(jnp.int32, (tq, tk), 1)
             < lax.broadcasted_iota(jnp.int32, (tq, tk), 0))
    after = (lax.broadcasted_iota(jnp.int32, (2 * tk, tk), 0) % tk
             > lax.broadcasted_iota(jnp.int32, (2 * tk, tk), 1)).astype(BF16)

    acc_ref[...] = jnp.zeros_like(acc_ref)
    rest_ref[...] = jnp.zeros_like(rest_ref)

    def block(j, diag):
        k0 = pl.multiple_of(j * tk, tk)
        log_betas, log_rests, parts = [], [], []
        for u in range(nh):
            hd = slice(u * dh, (u + 1) * dh)
            z2 = lax.dot_general(q_ref[0, :, hd], k_ref[0, pl.ds(k0, tk), hd], NT_DIMS,
                                 preferred_element_type=F32) * (scale * LOG2E)
            neg_abs = lax.bitcast_convert_type(
                lax.bitcast_convert_type(z2, jnp.uint32) | jnp.uint32(0x80000000), F32)
            log_beta = jnp.minimum(z2, 0.0) - jnp.log2(1.0 + jnp.exp2(neg_abs))
            log_rest = log_beta - z2
            if diag:
                log_rest = jnp.where(below, log_rest, 0.0)
            hi = lax.bitcast_convert_type(
                lax.bitcast_convert_type(log_rest, jnp.uint32) & jnp.uint32(0xFFFF0000), F32)
            parts.append(jnp.concatenate([hi.astype(BF16), (log_rest - hi).astype(BF16)], axis=1))
            log_betas.append(log_beta)
            log_rests.append(log_rest)
        cum = jnp.dot(jnp.concatenate(parts, axis=0), after, preferred_element_type=F32)
        for u in range(nh):
            hd = slice(u * dh, (u + 1) * dh)
            rest = rest_ref[u]
            later = cum[u * tq:(u + 1) * tq] + jnp.concatenate([rest] * (tk // LANES), axis=1)
            wgt = jnp.exp2(log_betas[u] + later)
            if diag:
                wgt = jnp.where(below, wgt, 0.0)
            acc_ref[:, hd] += jnp.dot(wgt.astype(BF16), v_ref[0, pl.ds(k0, tk), hd], preferred_element_type=F32)
            rest_ref[u] = rest + jnp.sum(log_rests[u], axis=1, keepdims=True)

    def all_underflowed():
        worst = jnp.max(jnp.max(rest_ref[...], axis=0))
        return (worst <= -SB_UNDERFLOW_LOG2).astype(jnp.int32)

    block(i, True)

    def more(carry):
        jj, done = carry
        return jnp.logical_and(jj <= i, done == 0)

    def step(carry):
        jj, _ = carry
        block(i - jj, False)
        return jj + 1, all_underflowed()

    lax.while_loop(more, step, (jnp.int32(1), all_underflowed()))
    o_ref[0] = acc_ref[...].astype(o_ref.dtype)


def _sb_attention(qkv, n_heads):
    B, S, three_d = qkv.shape
    dh = three_d // (3 * n_heads)
    tq = min(S, 2 * LANES)
    nh = min(n_heads, 4)
    n_groups = n_heads // nh
    kern = functools.partial(_sb_attn_kernel, scale=float(dh ** -0.5), dh=dh)
    return pl.pallas_call(
        kern,
        out_shape=jax.ShapeDtypeStruct((B, S, n_heads * dh), BF16),
        grid=(B, n_groups, S // tq),
        in_specs=[pl.BlockSpec((1, tq, nh * dh), lambda b, g, i: (b, i, g)),
                  pl.BlockSpec((1, S, nh * dh), lambda b, g, i: (b, 0, n_groups + g)),
                  pl.BlockSpec((1, S, nh * dh), lambda b, g, i: (b, 0, 2 * n_groups + g))],
        out_specs=pl.BlockSpec((1, tq, nh * dh), lambda b, g, i: (b, i, g)),
        scratch_shapes=[pltpu.VMEM((tq, nh * dh), F32), pltpu.VMEM((nh, tq, LANES), F32)],
        compiler_params=_params(("parallel", "parallel", "parallel")),
        name="sb_attention",
    )(qkv, qkv, qkv)


def kernel(x, c, rel_bias, ada_w, ada_b, norm_g, a_w_in, a_q_norm, a_kv_norm, a_idx_k_norm, a_w_qb, a_w_idx_qb,
           a_w_uk, a_w_uv, a_w_out, b_w_in, b_w_out, mlp_w1, mlp_w2):
    B, S, D = x.shape
    depth = ada_w.shape[0]
    n_mixers = 2

    mod = _ada_mod(c, ada_w, ada_b)
    modr = mod.reshape(depth * B * 6, 1, D)

    def mod_row(layer, chunk):
        return lambda b: (layer * B + b) * 6 + chunk

    rq, rkv, di = a_q_norm.shape[1], a_kv_norm.shape[1], a_idx_k_norm.shape[1]
    n_a_heads, _, a_qk_dim = a_w_uk.shape[1:]
    n_idx_heads = a_w_idx_qb.shape[2] // di
    k_top = min(IDX_TOPK, S // 4)
    bias_tab = _bias_tiles(rel_bias)

    for layer in range(depth):
        j = layer // n_mixers
        g = norm_g[layer]
        if layer % n_mixers == 0:
            w_in = a_w_in[j].astype(BF16)
            wwi = jnp.pad(w_in[:, rq + rkv + di:], ((0, 0), (0, LANES - n_idx_heads)))
            q_abs, q_idx, c_kv, k_idx, w_idx = _dsa_proj(
                x, g[0], modr, mod_row(layer, 0), mod_row(layer, 1),
                w_in[:, :rq], w_in[:, rq:rq + rkv], w_in[:, rq + rkv:rq + rkv + di], wwi,
                a_q_norm[j], a_kv_norm[j], a_idx_k_norm[j],
                a_w_qb[j].astype(BF16), a_w_idx_qb[j].astype(BF16),
                jnp.swapaxes(a_w_uk[j], 1, 2).astype(BF16), n_idx_heads)
            mask = _indexer_mask(q_idx, k_idx, w_idx, k_top)
            y = _dsa_attention(q_abs, c_kv, mask, bias_tab, a_w_uv[j].astype(BF16), float(a_qk_dim ** -0.5))
            w_out = a_w_out[j].astype(BF16)
        else:
            qkv = _in_proj(x, g[0], modr, mod_row(layer, 0), mod_row(layer, 1), b_w_in[j].astype(BF16), BF16)
            y = _sb_attention(qkv, B_HEADS)
            w_out = b_w_out[j].astype(BF16)
        x = _out_proj(y, w_out, x, g[1], modr, mod_row(layer, 2))
        x = _mlp(x, g[2], g[3], modr, mod_row(layer, 3), mod_row(layer, 4), mod_row(layer, 5),
                 mlp_w1[layer].astype(BF16), mlp_w2[layer].astype(BF16))
    return x
```

```python
import functools
import math

import numpy as np
import jax
import jax.numpy as jnp
from jax import lax
from jax.experimental import pallas as pl
from jax.experimental.pallas import tpu as pltpu

F32 = jnp.float32
BF16 = jnp.bfloat16

RMS_EPS = 1e-6
B_HEADS = 16
IDX_TOPK = 256
REL_MAX_DIST = 128

LANES = 128
V7X_VMEM_LIMIT_BYTES = 56 * 1024 * 1024

MASK_NEG = -1e30
M_INIT = -1e29
INT_MIN = int(np.iinfo(np.int32).min)
LOG2E = math.log2(math.e)
SB_UNDERFLOW_LOG2 = 160.0

NT_DIMS = (((1,), (1,)), ((), ()))


def _params(semantics):
    return pltpu.CompilerParams(dimension_semantics=semantics,
                                vmem_limit_bytes=V7X_VMEM_LIMIT_BYTES)


def _rms(x, g):
    return x * lax.rsqrt(jnp.mean(x * x, axis=-1, keepdims=True) + RMS_EPS) * g


def _normmod(x, g, shift, scale):
    return _rms(x, g * (1.0 + scale)) + shift


def _t5_bucket_thresholds(num_buckets):
    max_exact = num_buckets // 2

    def bucket(d):
        if d < max_exact:
            return d
        large = max_exact + int(math.log(max(d, 1) / max_exact)
                                / math.log(REL_MAX_DIST / max_exact) * (num_buckets - max_exact))
        return min(large, num_buckets - 1)

    thr, d = [], 0
    for b in range(num_buckets):
        while bucket(d) < b:
            d += 1
        thr.append(d)
    return tuple(thr)


def _ada_kernel(c_ref, w_ref, b_ref, o_ref):
    c = c_ref[...]
    s = c * jax.nn.sigmoid(c)
    o_ref[0] = jnp.dot(s.astype(BF16), w_ref[0].astype(BF16), preferred_element_type=F32) + b_ref[0]


def _ada_mod(c, ada_w, ada_b):
    L, D, E = ada_w.shape
    B = c.shape[0]
    tn = min(E, 1024)
    return pl.pallas_call(
        _ada_kernel,
        out_shape=jax.ShapeDtypeStruct((L, B, E), F32),
        grid=(L, E // tn),
        in_specs=[pl.BlockSpec((B, D), lambda l, j: (0, 0)),
                  pl.BlockSpec((1, D, tn), lambda l, j: (l, 0, j)),
                  pl.BlockSpec((1, 1, tn), lambda l, j: (l, 0, j))],
        out_specs=pl.BlockSpec((1, B, tn), lambda l, j: (l, 0, j)),
        compiler_params=_params(("parallel", "parallel")),
        name="ada_mod",
    )(c, ada_w, ada_b.reshape(L, 1, E))


def _in_proj_kernel(x_ref, g_ref, sh_ref, sc_ref, w_ref, o_ref, h_ref):
    @pl.when(pl.program_id(2) == 0)
    def _():
        h_ref[...] = _normmod(x_ref[0], g_ref[...], sh_ref[0], sc_ref[0]).astype(BF16)

    o_ref[0] = jnp.dot(h_ref[...], w_ref[...], preferred_element_type=F32).astype(o_ref.dtype)


def _in_proj(x, g, modr, shift_row, scale_row, w, out_dtype):
    B, S, D = x.shape
    n_out = w.shape[1]
    tm = min(S, 512)
    tn = min(n_out, 1024)
    return pl.pallas_call(
        _in_proj_kernel,
        out_shape=jax.ShapeDtypeStruct((B, S, n_out), out_dtype),
        grid=(B, S // tm, n_out // tn),
        in_specs=[pl.BlockSpec((1, tm, D), lambda b, i, j: (b, i, 0)),
                  pl.BlockSpec((1, D), lambda b, i, j: (0, 0)),
                  pl.BlockSpec((1, 1, D), lambda b, i, j: (shift_row(b), 0, 0)),
                  pl.BlockSpec((1, 1, D), lambda b, i, j: (scale_row(b), 0, 0)),
                  pl.BlockSpec((D, tn), lambda b, i, j: (0, j))],
        out_specs=pl.BlockSpec((1, tm, tn), lambda b, i, j: (b, i, j)),
        scratch_shapes=[pltpu.VMEM((tm, D), BF16)],
        compiler_params=_params(("parallel", "parallel", "arbitrary")),
        name="in_proj",
    )(x, g.reshape(1, D), modr, modr, w)


def _out_proj_kernel(a_ref, w_ref, x_ref, g_ref, gate_ref, o_ref, acc_ref):
    k = pl.program_id(2)

    @pl.when(k == 0)
    def _():
        acc_ref[...] = jnp.zeros_like(acc_ref)

    acc_ref[...] += jnp.dot(a_ref[0], w_ref[...], preferred_element_type=F32)

    @pl.when(k == pl.num_programs(2) - 1)
    def _():
        o_ref[0] = x_ref[0] + gate_ref[0] * _rms(acc_ref[...], g_ref[...])


def _out_proj(a, w, x, g, modr, gate_row):
    B, S, D = x.shape
    kin = a.shape[2]
    tm = min(S, 512)
    tk = min(kin, 2048)
    return pl.pallas_call(
        _out_proj_kernel,
        out_shape=jax.ShapeDtypeStruct((B, S, D), F32),
        grid=(B, S // tm, kin // tk),
        in_specs=[pl.BlockSpec((1, tm, tk), lambda b, i, k: (b, i, k)),
                  pl.BlockSpec((tk, D), lambda b, i, k: (k, 0)),
                  pl.BlockSpec((1, tm, D), lambda b, i, k: (b, i, 0)),
                  pl.BlockSpec((1, D), lambda b, i, k: (0, 0)),
                  pl.BlockSpec((1, 1, D), lambda b, i, k: (gate_row(b), 0, 0))],
        out_specs=pl.BlockSpec((1, tm, D), lambda b, i, k: (b, i, 0)),
        scratch_shapes=[pltpu.VMEM((tm, D), F32)],
        compiler_params=_params(("parallel", "parallel", "arbitrary")),
        name="out_proj",
    )(a, w, x, g.reshape(1, D), modr)


def _mlp_kernel(x_ref, g_in_ref, sh_ref, sc_ref, w1_ref, w2_ref, g_out_ref, gate_ref, o_ref, h_ref, acc_ref):
    f = pl.program_id(2)

    @pl.when(f == 0)
    def _():
        h_ref[...] = _normmod(x_ref[0], g_in_ref[...], sh_ref[0], sc_ref[0]).astype(BF16)
        acc_ref[...] = jnp.zeros_like(acc_ref)

    a = jnp.dot(h_ref[...], w1_ref[...], preferred_element_type=F32)
    a = jnp.square(jnp.maximum(a, 0.0)).astype(BF16)
    acc_ref[...] += jnp.dot(a, w2_ref[...], preferred_element_type=F32)

    @pl.when(f == pl.num_programs(2) - 1)
    def _():
        o_ref[0] = x_ref[0] + gate_ref[0] * _rms(acc_ref[...], g_out_ref[...])


def _mlp(x, g_in, g_out, modr, shift_row, scale_row, gate_row, w1, w2):
    B, S, D = x.shape
    dff = w1.shape[1]
    tm = min(S, 512)
    tf = min(dff, 1024)
    return pl.pallas_call(
        _mlp_kernel,
        out_shape=jax.ShapeDtypeStruct((B, S, D), F32),
        grid=(B, S // tm, dff // tf),
        in_specs=[pl.BlockSpec((1, tm, D), lambda b, i, f: (b, i, 0)),
                  pl.BlockSpec((1, D), lambda b, i, f: (0, 0)),
                  pl.BlockSpec((1, 1, D), lambda b, i, f: (shift_row(b), 0, 0)),
                  pl.BlockSpec((1, 1, D), lambda b, i, f: (scale_row(b), 0, 0)),
                  pl.BlockSpec((D, tf), lambda b, i, f: (0, f)),
                  pl.BlockSpec((tf, D), lambda b, i, f: (f, 0)),
                  pl.BlockSpec((1, D), lambda b, i, f: (0, 0)),
                  pl.BlockSpec((1, 1, D), lambda b, i, f: (gate_row(b), 0, 0))],
        out_specs=pl.BlockSpec((1, tm, D), lambda b, i, f: (b, i, 0)),
        scratch_shapes=[pltpu.VMEM((tm, D), BF16), pltpu.VMEM((tm, D), F32)],
        compiler_params=_params(("parallel", "parallel", "arbitrary")),
        name="mlp",
    )(x, g_in.reshape(1, D), modr, modr, w1, w2, g_out.reshape(1, D), modr)


def _dsa_proj_kernel(x_ref, g_ref, sh_ref, sc_ref, wq_ref, wkv_ref, wki_ref, wwi_ref,
                     qn_ref, kvn_ref, kin_ref, wqb_ref, wiqb_ref, wuk_ref,
                     qa_ref, qi_ref, ckv_ref, ki_ref, wi_ref, *, w_idx_scale):
    h = _normmod(x_ref[0], g_ref[...], sh_ref[0], sc_ref[0]).astype(BF16)
    q_lat = _rms(jnp.dot(h, wq_ref[...], preferred_element_type=F32), qn_ref[...]).astype(BF16)
    ckv_ref[0] = _rms(jnp.dot(h, wkv_ref[...], preferred_element_type=F32), kvn_ref[...]).astype(BF16)

    ki = jnp.dot(h, wki_ref[...], preferred_element_type=F32)
    mu = jnp.mean(ki, axis=-1, keepdims=True)
    var = jnp.mean(jnp.square(ki - mu), axis=-1, keepdims=True)
    ki_ref[0] = ((ki - mu) * lax.rsqrt(var + RMS_EPS) * kin_ref[...]).astype(BF16)

    n_ih = wi_ref.shape[2]
    wi = jnp.dot(h, wwi_ref[...], preferred_element_type=F32)
    wi_ref[0] = wi[:, :n_ih] * w_idx_scale

    n_heads, dh = wuk_ref.shape[0], wuk_ref.shape[1]
    q = jnp.dot(q_lat, wqb_ref[...], preferred_element_type=F32)
    for hd in range(n_heads):
        qh = q[:, hd * dh:(hd + 1) * dh].astype(BF16)
        qa_ref[0, hd] = jnp.dot(qh, wuk_ref[hd], preferred_element_type=F32).astype(BF16)

    di = qi_ref.shape[3]
    qidx = jnp.dot(q_lat, wiqb_ref[...], preferred_element_type=F32)
    for hd in range(qi_ref.shape[1]):
        qi_ref[0, hd] = qidx[:, hd * di:(hd + 1) * di].astype(BF16)


def _dsa_proj(x, g, modr, shift_row, scale_row, wq, wkv, wki, wwi, q_norm, kv_norm, idx_k_norm,
              w_qb, w_idx_qb, w_uk_t, n_idx_heads):
    B, S, D = x.shape
    rq, rkv, di = wq.shape[1], wkv.shape[1], wki.shape[1]
    n_heads, dh, _ = w_uk_t.shape
    tm = min(S, 256)
    const = lambda *shape: pl.BlockSpec(shape, lambda b, i: (0,) * len(shape))
    kern = functools.partial(_dsa_proj_kernel, w_idx_scale=float(n_idx_heads ** -0.5 * di ** -0.5))
    return pl.pallas_call(
        kern,
        out_shape=(jax.ShapeDtypeStruct((B, n_heads, S, rkv), BF16),
                   jax.ShapeDtypeStruct((B, n_idx_heads, S, di), BF16),
                   jax.ShapeDtypeStruct((B, S, rkv), BF16),
                   jax.ShapeDtypeStruct((B, S, di), BF16),
                   jax.ShapeDtypeStruct((B, S, n_idx_heads), F32)),
        grid=(B, S // tm),
        in_specs=[pl.BlockSpec((1, tm, D), lambda b, i: (b, i, 0)),
                  const(1, D),
                  pl.BlockSpec((1, 1, D), lambda b, i: (shift_row(b), 0, 0)),
                  pl.BlockSpec((1, 1, D), lambda b, i: (scale_row(b), 0, 0)),
                  const(D, rq), const(D, rkv), const(D, di), const(D, LANES),
                  const(1, rq), const(1, rkv), const(1, di),
                  const(rq, n_heads * dh), const(rq, n_idx_heads * di), const(n_heads, dh, rkv)],
        out_specs=(pl.BlockSpec((1, n_heads, tm, rkv), lambda b, i: (b, 0, i, 0)),
                   pl.BlockSpec((1, n_idx_heads, tm, di), lambda b, i: (b, 0, i, 0)),
                   pl.BlockSpec((1, tm, rkv), lambda b, i: (b, i, 0)),
                   pl.BlockSpec((1, tm, di), lambda b, i: (b, i, 0)),
                   pl.BlockSpec((1, tm, n_idx_heads), lambda b, i: (b, i, 0))),
        compiler_params=_params(("parallel", "parallel")),
        name="dsa_proj",
    )(x, g.reshape(1, D), modr, modr, wq, wkv, wki, wwi,
      q_norm.reshape(1, rq), kv_norm.reshape(1, rkv), idx_k_norm.reshape(1, di),
      w_qb, w_idx_qb, w_uk_t)


def _mono_key(score):
    score = jnp.where(score == 0.0, 0.0, score)
    bits = lax.bitcast_convert_type(score, jnp.int32)
    return bits ^ ((bits >> 31) & 0x7FFFFFFF)


def _indexer_kernel(qi_ref, ki_ref, w_ref, o_ref, key_ref, *, k_top, chunk):
    i = pl.program_id(1)
    n_ih, tq, di = qi_ref.shape[1:]
    n_pieces_total = o_ref.shape[2]
    per_chunk = chunk // LANES

    q2 = qi_ref[0].reshape(n_ih * tq, di)
    w = w_ref[0]
    wcols = [w[:, h:h + 1] for h in range(n_ih)]
    n_chunks = (i * tq + tq - 1) // chunk + 1
    row_t = i * tq + lax.broadcasted_iota(jnp.int32, (tq, chunk), 0)
    col_l = lax.broadcasted_iota(jnp.int32, (tq, chunk), 1)

    def chunk_body(c, carry):
        kc = ki_ref[0, pl.ds(pl.multiple_of(c * chunk, chunk), chunk), :]
        d = lax.dot_general(q2, kc, NT_DIMS, preferred_element_type=F32)
        sc = jnp.zeros((tq, chunk), F32)
        for h in range(n_ih):
            sc = sc + jnp.maximum(d[h * tq:(h + 1) * tq], 0.0) * wcols[h]
        key = jnp.where(c * chunk + col_l <= row_t, _mono_key(sc), INT_MIN)
        for k in range(per_chunk):
            key_ref[c * per_chunk + k] = key[:, k * LANES:(k + 1) * LANES]
        return carry

    lax.fori_loop(0, n_chunks, chunk_body, 0)
    n_pieces = n_chunks * per_chunk

    def count(pred):
        def body(c, cnt):
            for k in range(per_chunk):
                cnt = cnt + jnp.where(pred(key_ref[c * per_chunk + k]), 1.0, 0.0)
            return cnt
        cnt = lax.fori_loop(0, n_chunks, body, jnp.zeros((tq, LANES), F32))
        return jnp.broadcast_to(jnp.sum(cnt, axis=1, keepdims=True), (tq, LANES))

    kf = float(k_top)
    zero = jnp.zeros((tq, LANES), jnp.int32)
    prefix = jnp.where(count(lambda k: k >= zero) >= kf, zero, INT_MIN)

    def bit_body(b, prefix):
        cand = prefix | jnp.left_shift(jnp.int32(1), 30 - b)
        return jnp.where(count(lambda k: k >= cand) >= kf, cand, prefix)

    tau = lax.fori_loop(0, 31, bit_body, prefix)
    n_ge = count(lambda k: k >= tau)
    tied = jnp.max(n_ge) > kf

    def write(p, val):
        for r in range(tq // LANES):
            o_ref[0, r, p] = val[r * LANES:(r + 1) * LANES]

    @pl.when(jnp.logical_not(tied))
    def _():
        def body(p, carry):
            key = key_ref[p]
            write(p, jnp.where((key >= tau) & (key != INT_MIN), 0.0, MASK_NEG))
            return carry
        lax.fori_loop(0, n_pieces, body, 0)

    @pl.when(tied)
    def _():
        room = kf - count(lambda k: k > tau)
        before = (lax.broadcasted_iota(jnp.int32, (LANES, LANES), 0)
                  < lax.broadcasted_iota(jnp.int32, (LANES, LANES), 1)).astype(BF16)

        def body(p, seen):
            key = key_ref[p]
            tie = (key == tau) & (key != INT_MIN)
            tie_f = jnp.where(tie, 1.0, 0.0)
            rank = seen + jnp.dot(tie_f.astype(BF16), before, preferred_element_type=F32)
            write(p, jnp.where((key > tau) | (tie & (rank < room)), 0.0, MASK_NEG))
            return seen + jnp.broadcast_to(jnp.sum(tie_f, axis=1, keepdims=True), (tq, LANES))
        lax.fori_loop(0, n_pieces, body, jnp.zeros((tq, LANES), F32))

    def fill(p, carry):
        write(p, jnp.full((tq, LANES), MASK_NEG, F32))
        return carry
    lax.fori_loop(n_pieces, n_pieces_total, fill, 0)


def _indexer_mask(q_idx, k_idx, w_idx, k_top):
    B, n_ih, S, di = q_idx.shape
    tq = LANES
    chunk = min(S, 512)
    n_pieces = S // LANES
    kern = functools.partial(_indexer_kernel, k_top=k_top, chunk=chunk)
    return pl.pallas_call(
        kern,
        out_shape=jax.ShapeDtypeStruct((B, S // LANES, n_pieces, LANES, LANES), F32),
        grid=(B, S // tq),
        in_specs=[pl.BlockSpec((1, n_ih, tq, di), lambda b, i: (b, 0, i, 0)),
                  pl.BlockSpec((1, S, di), lambda b, i: (b, 0, 0)),
                  pl.BlockSpec((1, tq, n_ih), lambda b, i: (b, i, 0))],
        out_specs=pl.BlockSpec((1, tq // LANES, n_pieces, LANES, LANES), lambda b, i: (b, i, 0, 0, 0)),
        scratch_shapes=[pltpu.VMEM((n_pieces, tq, LANES), jnp.int32)],
        compiler_params=_params(("parallel", "parallel")),
        name="indexer_mask",
    )(q_idx, k_idx, w_idx)


def _bias_tile_kernel(rb_ref, o_ref, *, thresholds):
    kind = pl.program_id(0)
    h = pl.program_id(1)
    nb = len(thresholds)
    d = (lax.broadcasted_iota(jnp.int32, (LANES, LANES), 0)
         - lax.broadcasted_iota(jnp.int32, (LANES, LANES), 1) + kind * LANES)
    d = jnp.maximum(d, 0)
    far = rb_ref[nb - 1, h]
    val = jnp.full((LANES, LANES), (rb_ref[0, h] - far) * LOG2E, F32)
    for b in range(1, nb):
        val = jnp.where(d >= thresholds[b], (rb_ref[b, h] - far) * LOG2E, val)
    o_ref[0, 0] = jnp.where(kind == 2, 0.0, val)


def _bias_tiles(rel_bias):
    nb, n_heads = rel_bias.shape
    thresholds = _t5_bucket_thresholds(nb)
    assert thresholds[nb - 1] <= LANES + 1, "bias must saturate within one 128-key block"
    return pl.pallas_call(
        functools.partial(_bias_tile_kernel, thresholds=thresholds),
        out_shape=jax.ShapeDtypeStruct((3, n_heads, LANES, LANES), F32),
        grid=(3, n_heads),
        in_specs=[pl.BlockSpec(memory_space=pltpu.SMEM)],
        out_specs=pl.BlockSpec((1, 1, LANES, LANES), lambda k, h: (k, h, 0, 0)),
        compiler_params=_params(("parallel", "parallel")),
        name="bias_tiles",
    )(rel_bias)


def _dsa_attn_kernel(qa_ref, ckv_ref, mk_ref, tab_ref, wuv_ref, o_ref,
                     acc_ref, m_ref, l_ref, s0_ref, s1_ref, mb0_ref, mb1_ref, p_ref, a_ref, *, tk, unroll):
    i = pl.program_id(1)
    n_heads, tq, c = qa_ref.shape[1:]
    dv = wuv_ref.shape[2]
    npk = tk // LANES

    acc_ref[...] = jnp.zeros_like(acc_ref)
    m_ref[...] = jnp.full_like(m_ref, M_INIT)
    l_ref[...] = jnp.zeros_like(l_ref)

    def tile_kind(dist):
        return jnp.where(dist == 0, 0, jnp.where(dist == 1, 1, 2))

    def lanes(v, n):
        return jnp.concatenate([v] * (n // LANES), axis=1)

    n_groups = n_heads // unroll
    half = unroll // 2

    def split(n):
        return n // n_groups, pl.multiple_of((n % n_groups) * unroll, unroll)

    def scores(n, s_ref, mb_ref, with_bias):
        j, h0 = split(n)
        p0 = npk * j
        kv = ckv_ref[0, pl.ds(pl.multiple_of(j * tk, tk), tk), :]
        madd = jnp.concatenate([mk_ref[0, 0, p0 + k] for k in range(npk)], axis=1)
        for a in range(2):
            q = qa_ref[0, pl.ds(h0 + a * half, half)].reshape(half * tq, c)
            sa = lax.dot_general(q, kv, NT_DIMS, preferred_element_type=F32)
            for u in range(a * half, (a + 1) * half):
                s = sa[(u - a * half) * tq:(u - a * half + 1) * tq] + madd
                if with_bias:
                    s = s + jnp.concatenate(
                        [tab_ref.at[tile_kind(i - (p0 + k)), pl.ds(h0, unroll)][u] for k in range(npk)], axis=1)
                s_ref[u * tq:(u + 1) * tq, :] = s
                mb_ref[u] = jnp.broadcast_to(jnp.max(s, axis=1, keepdims=True), (tq, LANES))

    def reduce(n, s_ref, mb_ref):
        j, h0 = split(n)
        acc_g = acc_ref.at[pl.ds(h0, unroll)]
        m_g, l_g = m_ref.at[pl.ds(h0, unroll)], l_ref.at[pl.ds(h0, unroll)]
        kv = ckv_ref[0, pl.ds(pl.multiple_of(j * tk, tk), tk), :]
        for u in range(unroll):
            m_old = m_g[u]
            m_new = jnp.maximum(m_old, mb_ref[u])
            alpha = jnp.exp2(m_old - m_new)
            p = jnp.exp2(s_ref[u * tq:(u + 1) * tq, :] - lanes(m_new, tk))
            l_g[u] = alpha * l_g[u] + jnp.sum(p, axis=1, keepdims=True)
            m_g[u] = m_new
            p_ref[u * tq:(u + 1) * tq, :] = p.astype(BF16)
            a_ref[u] = alpha
        for a in range(2):
            pv = jnp.dot(p_ref[a * half * tq:(a + 1) * half * tq, :], kv, preferred_element_type=F32)
            for u in range(a * half, (a + 1) * half):
                acc_g[u] = lanes(a_ref[u], c) * acc_g[u] + pv[(u - a * half) * tq:(u - a * half + 1) * tq]

    def run(n_lo, n_hi, with_bias):
        @pl.when(n_hi > n_lo)
        def _():
            scores(n_lo, s0_ref, mb0_ref, with_bias)

            def pair(k, carry):
                n = n_lo + 2 * k
                scores(n + 1, s1_ref, mb1_ref, with_bias)
                reduce(n, s0_ref, mb0_ref)
                scores(jnp.minimum(n + 2, n_hi - 1), s0_ref, mb0_ref, with_bias)
                reduce(n + 1, s1_ref, mb1_ref)
                return carry

            lax.fori_loop(0, (n_hi - n_lo) // 2, pair, 0)

    n_kb = (i * tq + tq - 1) // tk + 1
    first_near = jnp.maximum((i - npk + 1) // npk, 0)
    run(0, first_near * n_groups, False)
    run(first_near * n_groups, n_kb * n_groups, True)

    for h in range(n_heads):
        o_lat = (acc_ref[h] * lanes(1.0 / l_ref[h], c)).astype(BF16)
        o_ref[0, :, h * dv:(h + 1) * dv] = jnp.dot(o_lat, wuv_ref[h], preferred_element_type=F32).astype(BF16)


def _dsa_attention(q_abs, c_kv, mask, bias_tab, w_uv):
    B, n_heads, S, c = q_abs.shape
    dv = w_uv.shape[2]
    tq = LANES
    tk = min(S, 2 * LANES)
    n_pieces = S // LANES
    unroll = min(n_heads, 8)
    assert (n_heads // unroll) % 2 == 0 and unroll % 2 == 0
    kern = functools.partial(_dsa_attn_kernel, tk=tk, unroll=unroll)
    return pl.pallas_call(
        kern,
        out_shape=jax.ShapeDtypeStruct((B, S, n_heads * dv), BF16),
        grid=(B, S // tq),
        in_specs=[pl.BlockSpec((1, n_heads, tq, c), lambda b, i: (b, 0, i, 0)),
                  pl.BlockSpec((1, S, c), lambda b, i: (b, 0, 0)),
                  pl.BlockSpec((1, 1, n_pieces, tq, LANES), lambda b, i: (b, i, 0, 0, 0)),
                  pl.BlockSpec((3, n_heads, LANES, LANES), lambda b, i: (0, 0, 0, 0)),
                  pl.BlockSpec((n_heads, c, dv), lambda b, i: (0, 0, 0))],
        out_specs=pl.BlockSpec((1, tq, n_heads * dv), lambda b, i: (b, i, 0)),
        scratch_shapes=[pltpu.VMEM((n_heads, tq, c), F32),
                        pltpu.VMEM((n_heads, tq, LANES), F32),
                        pltpu.VMEM((n_heads, tq, LANES), F32),
                        pltpu.VMEM((unroll * tq, tk), F32),
                        pltpu.VMEM((unroll * tq, tk), F32),
                        pltpu.VMEM((unroll, tq, LANES), F32),
                        pltpu.VMEM((unroll, tq, LANES), F32),
                        pltpu.VMEM((unroll * tq, tk), BF16),
                        pltpu.VMEM((unroll, tq, LANES), F32)],
        compiler_params=_params(("parallel", "parallel")),
        name="dsa_attention",
    )(q_abs, c_kv, mask, bias_tab, w_uv)


def _sb_attn_kernel(q_ref, k_ref, v_ref, o_ref, acc_ref, rest_ref, *, dh):
    i = pl.program_id(2)
    tq = q_ref.shape[1]
    tk = tq
    nh = q_ref.shape[2] // dh
    below = (lax.broadcasted_iota(jnp.int32, (tq, tk), 1)
             < lax.broadcasted_iota(jnp.int32, (tq, tk), 0))
    from_s = (lax.broadcasted_iota(jnp.int32, (2 * tk, tk), 0) % tk
              >= lax.broadcasted_iota(jnp.int32, (2 * tk, tk), 1)).astype(BF16)

    acc_ref[...] = jnp.zeros_like(acc_ref)
    rest_ref[...] = jnp.zeros_like(rest_ref)


    def block(j, diag):
        k0 = pl.multiple_of(j * tk, tk)
        z2s, log_rests, parts = [], [], []
        for u in range(nh):
            hd = slice(u * dh, (u + 1) * dh)
            z2 = lax.dot_general(q_ref[0, :, hd], k_ref[0, pl.ds(k0, tk), hd], NT_DIMS,
                                 preferred_element_type=F32)
            neg_abs = lax.bitcast_convert_type(
                lax.bitcast_convert_type(z2, jnp.uint32) | jnp.uint32(0x80000000), F32)
            log_rest = jnp.log(1.0 + jnp.exp2(neg_abs)) * (-LOG2E) - jnp.maximum(z2, 0.0)
            if diag:
                log_rest = jnp.where(below, log_rest, 0.0)
            hi = lax.bitcast_convert_type(
                lax.bitcast_convert_type(log_rest, jnp.uint32) & jnp.uint32(0xFFFF0000), F32)
            parts.append(jnp.concatenate([hi.astype(BF16), (log_rest - hi).astype(BF16)], axis=1))
            z2s.append(z2)
            log_rests.append(log_rest)
        cum = jnp.dot(jnp.concatenate(parts, axis=0), from_s, preferred_element_type=F32)
        for u in range(nh):
            hd = slice(u * dh, (u + 1) * dh)
            rest = rest_ref[u]
            wgt = jnp.exp2(z2s[u] + cum[u * tq:(u + 1) * tq] + jnp.concatenate([rest] * (tk // LANES), axis=1))
            if diag:
                wgt = jnp.where(below, wgt, 0.0)
            acc_ref[:, hd] += jnp.dot(wgt.astype(BF16), v_ref[0, pl.ds(k0, tk), hd], preferred_element_type=F32)
            rest_ref[u] = rest + jnp.sum(log_rests[u], axis=1, keepdims=True)

    def all_underflowed():
        worst = jnp.max(jnp.max(rest_ref[...], axis=0))
        return (worst <= -SB_UNDERFLOW_LOG2).astype(jnp.int32)

    block(i, True)

    def more(carry):
        jj, done = carry
        return jnp.logical_and(jj <= i, done == 0)

    def step(carry):
        jj, _ = carry
        block(i - jj, False)
        return jj + 1, all_underflowed()

    lax.while_loop(more, step, (jnp.int32(1), all_underflowed()))
    o_ref[0] = acc_ref[...].astype(o_ref.dtype)


def _sb_attention(qkv, n_heads):
    B, S, three_d = qkv.shape
    dh = three_d // (3 * n_heads)
    tq = min(S, 2 * LANES)
    nh = min(n_heads, 4)
    n_groups = n_heads // nh
    kern = functools.partial(_sb_attn_kernel, dh=dh)
    return pl.pallas_call(
        kern,
        out_shape=jax.ShapeDtypeStruct((B, S, n_heads * dh), BF16),
        grid=(B, n_groups, S // tq),
        in_specs=[pl.BlockSpec((1, tq, nh * dh), lambda b, g, i: (b, i, g)),
                  pl.BlockSpec((1, S, nh * dh), lambda b, g, i: (b, 0, n_groups + g)),
                  pl.BlockSpec((1, S, nh * dh), lambda b, g, i: (b, 0, 2 * n_groups + g))],
        out_specs=pl.BlockSpec((1, tq, nh * dh), lambda b, g, i: (b, i, g)),
        scratch_shapes=[pltpu.VMEM((tq, nh * dh), F32), pltpu.VMEM((nh, tq, LANES), F32)],
        compiler_params=_params(("parallel", "parallel", "parallel")),
        name="sb_attention",
    )(qkv, qkv, qkv)


def kernel(x, c, rel_bias, ada_w, ada_b, norm_g, a_w_in, a_q_norm, a_kv_norm, a_idx_k_norm, a_w_qb, a_w_idx_qb,
           a_w_uk, a_w_uv, a_w_out, b_w_in, b_w_out, mlp_w1, mlp_w2):
    B, S, D = x.shape
    depth = ada_w.shape[0]
    n_mixers = 2

    mod = _ada_mod(c, ada_w, ada_b)
    modr = mod.reshape(depth * B * 6, 1, D)

    def mod_row(layer, chunk):
        return lambda b: (layer * B + b) * 6 + chunk

    rq, rkv, di = a_q_norm.shape[1], a_kv_norm.shape[1], a_idx_k_norm.shape[1]
    n_a_heads, _, a_qk_dim = a_w_uk.shape[1:]
    n_idx_heads = a_w_idx_qb.shape[2] // di
    k_top = min(IDX_TOPK, S // 4)
    bias_tab = _bias_tiles(rel_bias)

    for layer in range(depth):
        j = layer // n_mixers
        g = norm_g[layer]
        if layer % n_mixers == 0:
            w_in = a_w_in[j].astype(BF16)
            wwi = jnp.pad(w_in[:, rq + rkv + di:], ((0, 0), (0, LANES - n_idx_heads)))
            q_abs, q_idx, c_kv, k_idx, w_idx = _dsa_proj(
                x, g[0], modr, mod_row(layer, 0), mod_row(layer, 1),
                w_in[:, :rq], w_in[:, rq:rq + rkv], w_in[:, rq + rkv:rq + rkv + di], wwi,
                a_q_norm[j], a_kv_norm[j], a_idx_k_norm[j],
                a_w_qb[j].astype(BF16), a_w_idx_qb[j].astype(BF16),
                (jnp.swapaxes(a_w_uk[j], 1, 2) * (a_qk_dim ** -0.5 * LOG2E)).astype(BF16), n_idx_heads)
            mask = _indexer_mask(q_idx, k_idx, w_idx, k_top)
            y = _dsa_attention(q_abs, c_kv, mask, bias_tab, a_w_uv[j].astype(BF16))
            w_out = a_w_out[j].astype(BF16)
        else:
            d_sb = b_w_in.shape[2] // 3
            col_scale = jnp.where(jnp.arange(3 * d_sb) < d_sb, (d_sb // B_HEADS) ** -0.5 * LOG2E, 1.0)
            qkv = _in_proj(x, g[0], modr, mod_row(layer, 0), mod_row(layer, 1),
                           (b_w_in[j] * col_scale).astype(BF16), BF16)
            y = _sb_attention(qkv, B_HEADS)
            w_out = b_w_out[j].astype(BF16)
        x = _out_proj(y, w_out, x, g[1], modr, mod_row(layer, 2))
        x = _mlp(x, g[2], g[3], modr, mod_row(layer, 3), mod_row(layer, 4), mod_row(layer, 5),
                 mlp_w1[layer].astype(BF16), mlp_w2[layer].astype(BF16))
    return x
```

```python
import functools
import math

import numpy as np
import jax
import jax.numpy as jnp
from jax import lax
from jax.experimental import pallas as pl
from jax.experimental.pallas import tpu as pltpu

F32 = jnp.float32
BF16 = jnp.bfloat16

RMS_EPS = 1e-6
B_HEADS = 16
IDX_TOPK = 256
REL_MAX_DIST = 128

LANES = 128
V7X_VMEM_LIMIT_BYTES = 56 * 1024 * 1024

MASK_NEG = -1e30
M_INIT = -1e29
INT_MIN = int(np.iinfo(np.int32).min)
LOG2E = math.log2(math.e)
SB_UNDERFLOW_LOG2 = 160.0

NT_DIMS = (((1,), (1,)), ((), ()))


def _params(semantics):
    return pltpu.CompilerParams(dimension_semantics=semantics,
                                vmem_limit_bytes=V7X_VMEM_LIMIT_BYTES)


def _rms(x, g):
    return x * lax.rsqrt(jnp.mean(x * x, axis=-1, keepdims=True) + RMS_EPS) * g


def _normmod(x, g, shift, scale):
    return _rms(x, g * (1.0 + scale)) + shift


def _t5_bucket_thresholds(num_buckets):
    max_exact = num_buckets // 2

    def bucket(d):
        if d < max_exact:
            return d
        large = max_exact + int(math.log(max(d, 1) / max_exact)
                                / math.log(REL_MAX_DIST / max_exact) * (num_buckets - max_exact))
        return min(large, num_buckets - 1)

    thr, d = [], 0
    for b in range(num_buckets):
        while bucket(d) < b:
            d += 1
        thr.append(d)
    return tuple(thr)


def _ada_kernel(c_ref, w_ref, b_ref, o_ref):
    c = c_ref[...]
    s = c * jax.nn.sigmoid(c)
    o_ref[0] = jnp.dot(s.astype(BF16), w_ref[0].astype(BF16), preferred_element_type=F32) + b_ref[0]


def _ada_mod(c, ada_w, ada_b):
    L, D, E = ada_w.shape
    B = c.shape[0]
    tn = min(E, 1024)
    return pl.pallas_call(
        _ada_kernel,
        out_shape=jax.ShapeDtypeStruct((L, B, E), F32),
        grid=(L, E // tn),
        in_specs=[pl.BlockSpec((B, D), lambda l, j: (0, 0)),
                  pl.BlockSpec((1, D, tn), lambda l, j: (l, 0, j)),
                  pl.BlockSpec((1, 1, tn), lambda l, j: (l, 0, j))],
        out_specs=pl.BlockSpec((1, B, tn), lambda l, j: (l, 0, j)),
        compiler_params=_params(("parallel", "parallel")),
        name="ada_mod",
    )(c, ada_w, ada_b.reshape(L, 1, E))


def _in_proj_kernel(x_ref, g_ref, sh_ref, sc_ref, w_ref, o_ref, h_ref):
    @pl.when(pl.program_id(2) == 0)
    def _():
        h_ref[...] = _normmod(x_ref[0], g_ref[...], sh_ref[0], sc_ref[0]).astype(BF16)

    o_ref[0] = jnp.dot(h_ref[...], w_ref[...], preferred_element_type=F32).astype(o_ref.dtype)


def _in_proj(x, g, modr, shift_row, scale_row, w, out_dtype):
    B, S, D = x.shape
    n_out = w.shape[1]
    tm = min(S, 512)
    tn = min(n_out, 2048)
    return pl.pallas_call(
        _in_proj_kernel,
        out_shape=jax.ShapeDtypeStruct((B, S, n_out), out_dtype),
        grid=(B, S // tm, n_out // tn),
        in_specs=[pl.BlockSpec((1, tm, D), lambda b, i, j: (b, i, 0)),
                  pl.BlockSpec((1, D), lambda b, i, j: (0, 0)),
                  pl.BlockSpec((1, 1, D), lambda b, i, j: (shift_row(b), 0, 0)),
                  pl.BlockSpec((1, 1, D), lambda b, i, j: (scale_row(b), 0, 0)),
                  pl.BlockSpec((D, tn), lambda b, i, j: (0, j))],
        out_specs=pl.BlockSpec((1, tm, tn), lambda b, i, j: (b, i, j)),
        scratch_shapes=[pltpu.VMEM((tm, D), BF16)],
        compiler_params=_params(("parallel", "parallel", "arbitrary")),
        name="in_proj",
    )(x, g.reshape(1, D), modr, modr, w)


def _out_proj_kernel(a_ref, w_ref, x_ref, g_ref, gate_ref, o_ref):
    y = jnp.dot(a_ref[0], w_ref[...], preferred_element_type=F32)
    o_ref[0] = x_ref[0] + gate_ref[0] * _rms(y, g_ref[...])


def _out_proj(a, w, x, g, modr, gate_row):
    B, S, D = x.shape
    kin = a.shape[2]
    w_bytes = 2 * kin * D * w.dtype.itemsize
    tm = min(S, 512 if w_bytes <= V7X_VMEM_LIMIT_BYTES // 3 else 256)
    return pl.pallas_call(
        _out_proj_kernel,
        out_shape=jax.ShapeDtypeStruct((B, S, D), F32),
        grid=(B, S // tm),
        in_specs=[pl.BlockSpec((1, tm, kin), lambda b, i: (b, i, 0)),
                  pl.BlockSpec((kin, D), lambda b, i: (0, 0)),
                  pl.BlockSpec((1, tm, D), lambda b, i: (b, i, 0)),
                  pl.BlockSpec((1, D), lambda b, i: (0, 0)),
                  pl.BlockSpec((1, 1, D), lambda b, i: (gate_row(b), 0, 0))],
        out_specs=pl.BlockSpec((1, tm, D), lambda b, i: (b, i, 0)),
        compiler_params=_params(("parallel", "parallel")),
        name="out_proj",
    )(a, w, x, g.reshape(1, D), modr)


def _mlp_kernel(x_ref, g_in_ref, sh_ref, sc_ref, w1_ref, w2_ref, g_out_ref, gate_ref, o_ref, h_ref, acc_ref):
    f = pl.program_id(2)

    @pl.when(f == 0)
    def _():
        h_ref[...] = _normmod(x_ref[0], g_in_ref[...], sh_ref[0], sc_ref[0]).astype(BF16)
        acc_ref[...] = jnp.zeros_like(acc_ref)

    a = jnp.dot(h_ref[...], w1_ref[...], preferred_element_type=F32)
    a = jnp.square(jnp.maximum(a, 0.0)).astype(BF16)
    acc_ref[...] += jnp.dot(a, w2_ref[...], preferred_element_type=F32)

    @pl.when(f == pl.num_programs(2) - 1)
    def _():
        o_ref[0] = x_ref[0] + gate_ref[0] * _rms(acc_ref[...], g_out_ref[...])


def _mlp(x, g_in, g_out, modr, shift_row, scale_row, gate_row, w1, w2):
    B, S, D = x.shape
    dff = w1.shape[1]
    tm = min(S, 512)
    tf = min(dff, 1024)
    return pl.pallas_call(
        _mlp_kernel,
        out_shape=jax.ShapeDtypeStruct((B, S, D), F32),
        grid=(B, S // tm, dff // tf),
        in_specs=[pl.BlockSpec((1, tm, D), lambda b, i, f: (b, i, 0)),
                  pl.BlockSpec((1, D), lambda b, i, f: (0, 0)),
                  pl.BlockSpec((1, 1, D), lambda b, i, f: (shift_row(b), 0, 0)),
                  pl.BlockSpec((1, 1, D), lambda b, i, f: (scale_row(b), 0, 0)),
                  pl.BlockSpec((D, tf), lambda b, i, f: (0, f)),
                  pl.BlockSpec((tf, D), lambda b, i, f: (f, 0)),
                  pl.BlockSpec((1, D), lambda b, i, f: (0, 0)),
                  pl.BlockSpec((1, 1, D), lambda b, i, f: (gate_row(b), 0, 0))],
        out_specs=pl.BlockSpec((1, tm, D), lambda b, i, f: (b, i, 0)),
        scratch_shapes=[pltpu.VMEM((tm, D), BF16), pltpu.VMEM((tm, D), F32)],
        compiler_params=_params(("parallel", "parallel", "arbitrary")),
        name="mlp",
    )(x, g_in.reshape(1, D), modr, modr, w1, w2, g_out.reshape(1, D), modr)


def _dsa_proj_kernel(x_ref, g_ref, sh_ref, sc_ref, wq_ref, wkv_ref, wki_ref, wwi_ref,
                     qn_ref, kvn_ref, kin_ref, wqb_ref, wiqb_ref, wuk_ref,
                     qa_ref, qi_ref, ckv_ref, ki_ref, wi_ref, *, w_idx_scale):
    h = _normmod(x_ref[0], g_ref[...], sh_ref[0], sc_ref[0]).astype(BF16)
    q_lat = _rms(jnp.dot(h, wq_ref[...], preferred_element_type=F32), qn_ref[...]).astype(BF16)
    ckv_ref[0] = _rms(jnp.dot(h, wkv_ref[...], preferred_element_type=F32), kvn_ref[...]).astype(BF16)

    ki = jnp.dot(h, wki_ref[...], preferred_element_type=F32)
    mu = jnp.mean(ki, axis=-1, keepdims=True)
    var = jnp.mean(jnp.square(ki - mu), axis=-1, keepdims=True)
    ki_ref[0] = ((ki - mu) * lax.rsqrt(var + RMS_EPS) * kin_ref[...]).astype(BF16)

    n_ih = wi_ref.shape[2]
    wi = jnp.dot(h, wwi_ref[...], preferred_element_type=F32)
    wi_ref[0] = wi[:, :n_ih] * w_idx_scale

    n_heads, dh = wuk_ref.shape[0], wuk_ref.shape[1]
    q = jnp.dot(q_lat, wqb_ref[...], preferred_element_type=F32)
    for hd in range(n_heads):
        qh = q[:, hd * dh:(hd + 1) * dh].astype(BF16)
        qa_ref[0, hd] = jnp.dot(qh, wuk_ref[hd], preferred_element_type=F32).astype(BF16)

    di = qi_ref.shape[3]
    qidx = jnp.dot(q_lat, wiqb_ref[...], preferred_element_type=F32)
    for hd in range(qi_ref.shape[1]):
        qi_ref[0, hd] = qidx[:, hd * di:(hd + 1) * di].astype(BF16)


def _dsa_proj(x, g, modr, shift_row, scale_row, wq, wkv, wki, wwi, q_norm, kv_norm, idx_k_norm,
              w_qb, w_idx_qb, w_uk_t, n_idx_heads):
    B, S, D = x.shape
    rq, rkv, di = wq.shape[1], wkv.shape[1], wki.shape[1]
    n_heads, dh, _ = w_uk_t.shape
    tm = min(S, 256)
    const = lambda *shape: pl.BlockSpec(shape, lambda b, i: (0,) * len(shape))
    kern = functools.partial(_dsa_proj_kernel, w_idx_scale=float(n_idx_heads ** -0.5 * di ** -0.5))
    return pl.pallas_call(
        kern,
        out_shape=(jax.ShapeDtypeStruct((B, n_heads, S, rkv), BF16),
                   jax.ShapeDtypeStruct((B, n_idx_heads, S, di), BF16),
                   jax.ShapeDtypeStruct((B, S, rkv), BF16),
                   jax.ShapeDtypeStruct((B, S, di), BF16),
                   jax.ShapeDtypeStruct((B, S, n_idx_heads), F32)),
        grid=(B, S // tm),
        in_specs=[pl.BlockSpec((1, tm, D), lambda b, i: (b, i, 0)),
                  const(1, D),
                  pl.BlockSpec((1, 1, D), lambda b, i: (shift_row(b), 0, 0)),
                  pl.BlockSpec((1, 1, D), lambda b, i: (scale_row(b), 0, 0)),
                  const(D, rq), const(D, rkv), const(D, di), const(D, LANES),
                  const(1, rq), const(1, rkv), const(1, di),
                  const(rq, n_heads * dh), const(rq, n_idx_heads * di), const(n_heads, dh, rkv)],
        out_specs=(pl.BlockSpec((1, n_heads, tm, rkv), lambda b, i: (b, 0, i, 0)),
                   pl.BlockSpec((1, n_idx_heads, tm, di), lambda b, i: (b, 0, i, 0)),
                   pl.BlockSpec((1, tm, rkv), lambda b, i: (b, i, 0)),
                   pl.BlockSpec((1, tm, di), lambda b, i: (b, i, 0)),
                   pl.BlockSpec((1, tm, n_idx_heads), lambda b, i: (b, i, 0))),
        compiler_params=_params(("parallel", "parallel")),
        name="dsa_proj",
    )(x, g.reshape(1, D), modr, modr, wq, wkv, wki, wwi,
      q_norm.reshape(1, rq), kv_norm.reshape(1, rkv), idx_k_norm.reshape(1, di),
      w_qb, w_idx_qb, w_uk_t)


def _mono_key(score):
    score = jnp.where(score == 0.0, 0.0, score)
    bits = lax.bitcast_convert_type(score, jnp.int32)
    return bits ^ ((bits >> 31) & 0x7FFFFFFF)


def _indexer_kernel(qi_ref, ki_ref, w_ref, o_ref, key_ref, *, k_top, chunk):
    i = pl.program_id(1)
    n_ih, tq, di = qi_ref.shape[1:]
    n_pieces_total = o_ref.shape[2]
    per_chunk = chunk // LANES

    q2 = qi_ref[0].reshape(n_ih * tq, di)
    w = w_ref[0]
    wcols = [w[:, h:h + 1] for h in range(n_ih)]
    n_chunks = (i * tq + tq - 1) // chunk + 1
    row_t = i * tq + lax.broadcasted_iota(jnp.int32, (tq, chunk), 0)
    col_l = lax.broadcasted_iota(jnp.int32, (tq, chunk), 1)

    def chunk_body(c, carry):
        kc = ki_ref[0, pl.ds(pl.multiple_of(c * chunk, chunk), chunk), :]
        d = lax.dot_general(q2, kc, NT_DIMS, preferred_element_type=F32)
        sc = jnp.zeros((tq, chunk), F32)
        for h in range(n_ih):
            sc = sc + jnp.maximum(d[h * tq:(h + 1) * tq], 0.0) * wcols[h]
        key = jnp.where(c * chunk + col_l <= row_t, _mono_key(sc), INT_MIN)
        for k in range(per_chunk):
            key_ref[c * per_chunk + k] = key[:, k * LANES:(k + 1) * LANES]
        return carry

    lax.fori_loop(0, n_chunks, chunk_body, 0)
    n_pieces = n_chunks * per_chunk

    def count(pred):
        def body(c, cnt):
            for k in range(per_chunk):
                cnt = cnt + jnp.where(pred(key_ref[c * per_chunk + k]), 1.0, 0.0)
            return cnt
        cnt = lax.fori_loop(0, n_chunks, body, jnp.zeros((tq, LANES), F32))
        return jnp.broadcast_to(jnp.sum(cnt, axis=1, keepdims=True), (tq, LANES))

    kf = float(k_top)
    zero = jnp.zeros((tq, LANES), jnp.int32)
    prefix = jnp.where(count(lambda k: k >= zero) >= kf, zero, INT_MIN)

    def bit_body(b, prefix):
        cand = prefix | jnp.left_shift(jnp.int32(1), 30 - b)
        return jnp.where(count(lambda k: k >= cand) >= kf, cand, prefix)

    tau = lax.fori_loop(0, 31, bit_body, prefix)
    n_ge = count(lambda k: k >= tau)
    tied = jnp.max(n_ge) > kf

    def write(p, val):
        for r in range(tq // LANES):
            o_ref[0, r, p] = val[r * LANES:(r + 1) * LANES]

    @pl.when(jnp.logical_not(tied))
    def _():
        def body(p, carry):
            key = key_ref[p]
            write(p, jnp.where((key >= tau) & (key != INT_MIN), 0.0, MASK_NEG))
            return carry
        lax.fori_loop(0, n_pieces, body, 0)

    @pl.when(tied)
    def _():
        room = kf - count(lambda k: k > tau)
        before = (lax.broadcasted_iota(jnp.int32, (LANES, LANES), 0)
                  < lax.broadcasted_iota(jnp.int32, (LANES, LANES), 1)).astype(BF16)

        def body(p, seen):
            key = key_ref[p]
            tie = (key == tau) & (key != INT_MIN)
            tie_f = jnp.where(tie, 1.0, 0.0)
            rank = seen + jnp.dot(tie_f.astype(BF16), before, preferred_element_type=F32)
            write(p, jnp.where((key > tau) | (tie & (rank < room)), 0.0, MASK_NEG))
            return seen + jnp.broadcast_to(jnp.sum(tie_f, axis=1, keepdims=True), (tq, LANES))
        lax.fori_loop(0, n_pieces, body, jnp.zeros((tq, LANES), F32))

    def fill(p, carry):
        write(p, jnp.full((tq, LANES), MASK_NEG, F32))
        return carry
    lax.fori_loop(n_pieces, n_pieces_total, fill, 0)


def _indexer_mask(q_idx, k_idx, w_idx, k_top):
    B, n_ih, S, di = q_idx.shape
    tq = LANES
    chunk = min(S, 512)
    n_pieces = S // LANES
    kern = functools.partial(_indexer_kernel, k_top=k_top, chunk=chunk)
    return pl.pallas_call(
        kern,
        out_shape=jax.ShapeDtypeStruct((B, S // LANES, n_pieces, LANES, LANES), F32),
        grid=(B, S // tq),
        in_specs=[pl.BlockSpec((1, n_ih, tq, di), lambda b, i: (b, 0, i, 0)),
                  pl.BlockSpec((1, S, di), lambda b, i: (b, 0, 0)),
                  pl.BlockSpec((1, tq, n_ih), lambda b, i: (b, i, 0))],
        out_specs=pl.BlockSpec((1, tq // LANES, n_pieces, LANES, LANES), lambda b, i: (b, i, 0, 0, 0)),
        scratch_shapes=[pltpu.VMEM((n_pieces, tq, LANES), jnp.int32)],
        compiler_params=_params(("parallel", "parallel")),
        name="indexer_mask",
    )(q_idx, k_idx, w_idx)


def _bias_tile_kernel(rb_ref, o_ref, *, thresholds):
    kind = pl.program_id(0)
    h = pl.program_id(1)
    nb = len(thresholds)
    d = (lax.broadcasted_iota(jnp.int32, (LANES, LANES), 0)
         - lax.broadcasted_iota(jnp.int32, (LANES, LANES), 1) + kind * LANES)
    d = jnp.maximum(d, 0)
    far = rb_ref[nb - 1, h]
    val = jnp.full((LANES, LANES), (rb_ref[0, h] - far) * LOG2E, F32)
    for b in range(1, nb):
        val = jnp.where(d >= thresholds[b], (rb_ref[b, h] - far) * LOG2E, val)
    o_ref[0, 0] = jnp.where(kind == 2, 0.0, val)


def _bias_tiles(rel_bias):
    nb, n_heads = rel_bias.shape
    thresholds = _t5_bucket_thresholds(nb)
    assert thresholds[nb - 1] <= LANES + 1, "bias must saturate within one 128-key block"
    return pl.pallas_call(
        functools.partial(_bias_tile_kernel, thresholds=thresholds),
        out_shape=jax.ShapeDtypeStruct((3, n_heads, LANES, LANES), F32),
        grid=(3, n_heads),
        in_specs=[pl.BlockSpec(memory_space=pltpu.SMEM)],
        out_specs=pl.BlockSpec((1, 1, LANES, LANES), lambda k, h: (k, h, 0, 0)),
        compiler_params=_params(("parallel", "parallel")),
        name="bias_tiles",
    )(rel_bias)


def _dsa_attn_kernel(qa_ref, ckv_ref, mk_ref, tab_ref, wuv_ref, o_ref,
                     acc_ref, m_ref, l_ref, s0_ref, s1_ref, mb0_ref, mb1_ref, p_ref, a_ref, *, tk, unroll):
    i = pl.program_id(1)
    n_heads, tq, c = qa_ref.shape[1:]
    dv = wuv_ref.shape[2]
    npk = tk // LANES

    acc_ref[...] = jnp.zeros_like(acc_ref)
    m_ref[...] = jnp.full_like(m_ref, M_INIT)
    l_ref[...] = jnp.zeros_like(l_ref)

    def tile_kind(dist):
        return jnp.where(dist == 0, 0, jnp.where(dist == 1, 1, 2))

    def lanes(v, n):
        return jnp.concatenate([v] * (n // LANES), axis=1)

    n_groups = n_heads // unroll
    half = unroll // 2

    def split(n):
        return n // n_groups, pl.multiple_of((n % n_groups) * unroll, unroll)

    def scores(n, s_ref, mb_ref, with_bias):
        j, h0 = split(n)
        p0 = npk * j
        kv = ckv_ref[0, pl.ds(pl.multiple_of(j * tk, tk), tk), :]
        madd = jnp.concatenate([mk_ref[0, 0, p0 + k] for k in range(npk)], axis=1)
        for a in range(2):
            q = qa_ref[0, pl.ds(h0 + a * half, half)].reshape(half * tq, c)
            sa = lax.dot_general(q, kv, NT_DIMS, preferred_element_type=F32)
            for u in range(a * half, (a + 1) * half):
                s = sa[(u - a * half) * tq:(u - a * half + 1) * tq] + madd
                if with_bias:
                    s = s + jnp.concatenate(
                        [tab_ref.at[tile_kind(i - (p0 + k)), pl.ds(h0, unroll)][u] for k in range(npk)], axis=1)
                s_ref[u * tq:(u + 1) * tq, :] = s
                mb_ref[u] = jnp.broadcast_to(jnp.max(s, axis=1, keepdims=True), (tq, LANES))

    def reduce(n, s_ref, mb_ref):
        j, h0 = split(n)
        acc_g = acc_ref.at[pl.ds(h0, unroll)]
        m_g, l_g = m_ref.at[pl.ds(h0, unroll)], l_ref.at[pl.ds(h0, unroll)]
        kv = ckv_ref[0, pl.ds(pl.multiple_of(j * tk, tk), tk), :]
        for u in range(unroll):
            m_old = m_g[u]
            m_new = jnp.maximum(m_old, mb_ref[u])
            alpha = jnp.exp2(m_old - m_new)
            p = jnp.exp2(s_ref[u * tq:(u + 1) * tq, :] - lanes(m_new, tk))
            l_g[u] = alpha * l_g[u] + jnp.sum(p, axis=1, keepdims=True)
            m_g[u] = m_new
            p_ref[u * tq:(u + 1) * tq, :] = p.astype(BF16)
            a_ref[u] = alpha
        for a in range(2):
            pv = jnp.dot(p_ref[a * half * tq:(a + 1) * half * tq, :], kv, preferred_element_type=F32)
            for u in range(a * half, (a + 1) * half):
                acc_g[u] = lanes(a_ref[u], c) * acc_g[u] + pv[(u - a * half) * tq:(u - a * half + 1) * tq]

    def run(n_lo, n_hi, with_bias):
        @pl.when(n_hi > n_lo)
        def _():
            scores(n_lo, s0_ref, mb0_ref, with_bias)

            def pair(k, carry):
                n = n_lo + 2 * k
                scores(n + 1, s1_ref, mb1_ref, with_bias)
                reduce(n, s0_ref, mb0_ref)
                scores(jnp.minimum(n + 2, n_hi - 1), s0_ref, mb0_ref, with_bias)
                reduce(n + 1, s1_ref, mb1_ref)
                return carry

            lax.fori_loop(0, (n_hi - n_lo) // 2, pair, 0)

    n_kb = (i * tq + tq - 1) // tk + 1
    first_near = jnp.maximum((i - npk + 1) // npk, 0)
    run(0, first_near * n_groups, False)
    run(first_near * n_groups, n_kb * n_groups, True)

    for h in range(n_heads):
        o_lat = (acc_ref[h] * lanes(1.0 / l_ref[h], c)).astype(BF16)
        o_ref[0, :, h * dv:(h + 1) * dv] = jnp.dot(o_lat, wuv_ref[h], preferred_element_type=F32).astype(BF16)


def _dsa_attention(q_abs, c_kv, mask, bias_tab, w_uv):
    B, n_heads, S, c = q_abs.shape
    dv = w_uv.shape[2]
    tq = LANES
    tk = min(S, 2 * LANES)
    n_pieces = S // LANES
    unroll = min(n_heads, 8)
    assert (n_heads // unroll) % 2 == 0 and unroll % 2 == 0
    kern = functools.partial(_dsa_attn_kernel, tk=tk, unroll=unroll)
    return pl.pallas_call(
        kern,
        out_shape=jax.ShapeDtypeStruct((B, S, n_heads * dv), BF16),
        grid=(B, S // tq),
        in_specs=[pl.BlockSpec((1, n_heads, tq, c), lambda b, i: (b, 0, i, 0)),
                  pl.BlockSpec((1, S, c), lambda b, i: (b, 0, 0)),
                  pl.BlockSpec((1, 1, n_pieces, tq, LANES), lambda b, i: (b, i, 0, 0, 0)),
                  pl.BlockSpec((3, n_heads, LANES, LANES), lambda b, i: (0, 0, 0, 0)),
                  pl.BlockSpec((n_heads, c, dv), lambda b, i: (0, 0, 0))],
        out_specs=pl.BlockSpec((1, tq, n_heads * dv), lambda b, i: (b, i, 0)),
        scratch_shapes=[pltpu.VMEM((n_heads, tq, c), F32),
                        pltpu.VMEM((n_heads, tq, LANES), F32),
                        pltpu.VMEM((n_heads, tq, LANES), F32),
                        pltpu.VMEM((unroll * tq, tk), F32),
                        pltpu.VMEM((unroll * tq, tk), F32),
                        pltpu.VMEM((unroll, tq, LANES), F32),
                        pltpu.VMEM((unroll, tq, LANES), F32),
                        pltpu.VMEM((unroll * tq, tk), BF16),
                        pltpu.VMEM((unroll, tq, LANES), F32)],
        compiler_params=_params(("parallel", "parallel")),
        name="dsa_attention",
    )(q_abs, c_kv, mask, bias_tab, w_uv)


def _sb_attn_kernel(q_ref, k_ref, v_ref, o_ref, acc_ref, rest_ref, *, dh):
    i = pl.program_id(2)
    tq = q_ref.shape[1]
    tk = tq
    nh = q_ref.shape[2] // dh
    below = (lax.broadcasted_iota(jnp.int32, (tq, tk), 1)
             < lax.broadcasted_iota(jnp.int32, (tq, tk), 0))
    from_s = (lax.broadcasted_iota(jnp.int32, (2 * tk, tk), 0) % tk
              >= lax.broadcasted_iota(jnp.int32, (2 * tk, tk), 1)).astype(BF16)

    acc_ref[...] = jnp.zeros_like(acc_ref)
    rest_ref[...] = jnp.zeros_like(rest_ref)


    def block(j, diag):
        k0 = pl.multiple_of(j * tk, tk)
        z2s, log_rests, parts = [], [], []
        for u in range(nh):
            hd = slice(u * dh, (u + 1) * dh)
            z2 = lax.dot_general(q_ref[0, :, hd], k_ref[0, pl.ds(k0, tk), hd], NT_DIMS,
                                 preferred_element_type=F32)
            neg_abs = lax.bitcast_convert_type(
                lax.bitcast_convert_type(z2, jnp.uint32) | jnp.uint32(0x80000000), F32)
            log_rest = jnp.log(1.0 + jnp.exp2(neg_abs)) * (-LOG2E) - jnp.maximum(z2, 0.0)
            if diag:
                log_rest = jnp.where(below, log_rest, 0.0)
            hi = lax.bitcast_convert_type(
                lax.bitcast_convert_type(log_rest, jnp.uint32) & jnp.uint32(0xFFFF0000), F32)
            parts.append(jnp.concatenate([hi.astype(BF16), (log_rest - hi).astype(BF16)], axis=1))
            z2s.append(z2)
            log_rests.append(log_rest)
        cum = jnp.dot(jnp.concatenate(parts, axis=0), from_s, preferred_element_type=F32)
        for u in range(nh):
            hd = slice(u * dh, (u + 1) * dh)
            rest = rest_ref[u]
            wgt = jnp.exp2(z2s[u] + cum[u * tq:(u + 1) * tq] + jnp.concatenate([rest] * (tk // LANES), axis=1))
            if diag:
                wgt = jnp.where(below, wgt, 0.0)
            acc_ref[:, hd] += jnp.dot(wgt.astype(BF16), v_ref[0, pl.ds(k0, tk), hd], preferred_element_type=F32)
            rest_ref[u] = rest + jnp.sum(log_rests[u], axis=1, keepdims=True)

    def all_underflowed():
        worst = jnp.max(jnp.max(rest_ref[...], axis=0))
        return (worst <= -SB_UNDERFLOW_LOG2).astype(jnp.int32)

    block(i, True)

    def more(carry):
        jj, done = carry
        return jnp.logical_and(jj <= i, done == 0)

    def step(carry):
        jj, _ = carry
        block(i - jj, False)
        return jj + 1, all_underflowed()

    lax.while_loop(more, step, (jnp.int32(1), all_underflowed()))
    o_ref[0] = acc_ref[...].astype(o_ref.dtype)


def _sb_attention(qkv, n_heads):
    B, S, three_d = qkv.shape
    dh = three_d // (3 * n_heads)
    tq = min(S, 2 * LANES)
    nh = min(n_heads, 4)
    n_groups = n_heads // nh
    kern = functools.partial(_sb_attn_kernel, dh=dh)
    return pl.pallas_call(
        kern,
        out_shape=jax.ShapeDtypeStruct((B, S, n_heads * dh), BF16),
        grid=(B, n_groups, S // tq),
        in_specs=[pl.BlockSpec((1, tq, nh * dh), lambda b, g, i: (b, i, g)),
                  pl.BlockSpec((1, S, nh * dh), lambda b, g, i: (b, 0, n_groups + g)),
                  pl.BlockSpec((1, S, nh * dh), lambda b, g, i: (b, 0, 2 * n_groups + g))],
        out_specs=pl.BlockSpec((1, tq, nh * dh), lambda b, g, i: (b, i, g)),
        scratch_shapes=[pltpu.VMEM((tq, nh * dh), F32), pltpu.VMEM((nh, tq, LANES), F32)],
        compiler_params=_params(("parallel", "parallel", "parallel")),
        name="sb_attention",
    )(qkv, qkv, qkv)


def kernel(x, c, rel_bias, ada_w, ada_b, norm_g, a_w_in, a_q_norm, a_kv_norm, a_idx_k_norm, a_w_qb, a_w_idx_qb,
           a_w_uk, a_w_uv, a_w_out, b_w_in, b_w_out, mlp_w1, mlp_w2):
    B, S, D = x.shape
    depth = ada_w.shape[0]
    n_mixers = 2

    mod = _ada_mod(c, ada_w, ada_b)
    modr = mod.reshape(depth * B * 6, 1, D)

    def mod_row(layer, chunk):
        return lambda b: (layer * B + b) * 6 + chunk

    rq, rkv, di = a_q_norm.shape[1], a_kv_norm.shape[1], a_idx_k_norm.shape[1]
    n_a_heads, _, a_qk_dim = a_w_uk.shape[1:]
    n_idx_heads = a_w_idx_qb.shape[2] // di
    k_top = min(IDX_TOPK, S // 4)
    bias_tab = _bias_tiles(rel_bias)

    for layer in range(depth):
        j = layer // n_mixers
        g = norm_g[layer]
        if layer % n_mixers == 0:
            w_in = a_w_in[j].astype(BF16)
            wwi = jnp.pad(w_in[:, rq + rkv + di:], ((0, 0), (0, LANES - n_idx_heads)))
            q_abs, q_idx, c_kv, k_idx, w_idx = _dsa_proj(
                x, g[0], modr, mod_row(layer, 0), mod_row(layer, 1),
                w_in[:, :rq], w_in[:, rq:rq + rkv], w_in[:, rq + rkv:rq + rkv + di], wwi,
                a_q_norm[j], a_kv_norm[j], a_idx_k_norm[j],
                a_w_qb[j].astype(BF16), a_w_idx_qb[j].astype(BF16),
                (jnp.swapaxes(a_w_uk[j], 1, 2) * (a_qk_dim ** -0.5 * LOG2E)).astype(BF16), n_idx_heads)
            mask = _indexer_mask(q_idx, k_idx, w_idx, k_top)
            y = _dsa_attention(q_abs, c_kv, mask, bias_tab, a_w_uv[j].astype(BF16))
            w_out = a_w_out[j].astype(BF16)
        else:
            d_sb = b_w_in.shape[2] // 3
            col_scale = jnp.where(jnp.arange(3 * d_sb) < d_sb, (d_sb // B_HEADS) ** -0.5 * LOG2E, 1.0)
            qkv = _in_proj(x, g[0], modr, mod_row(layer, 0), mod_row(layer, 1),
                           (b_w_in[j] * col_scale).astype(BF16), BF16)
            y = _sb_attention(qkv, B_HEADS)
            w_out = b_w_out[j].astype(BF16)
        x = _out_proj(y, w_out, x, g[1], modr, mod_row(layer, 2))
        x = _mlp(x, g[2], g[3], modr, mod_row(layer, 3), mod_row(layer, 4), mod_row(layer, 5),
                 mlp_w1[layer].astype(BF16), mlp_w2[layer].astype(BF16))
    return x
```

```python
import functools
import math

import numpy as np
import jax
import jax.numpy as jnp
from jax import lax
from jax.experimental import pallas as pl
from jax.experimental.pallas import tpu as pltpu

F32 = jnp.float32
BF16 = jnp.bfloat16

RMS_EPS = 1e-6
B_HEADS = 16
IDX_TOPK = 256
REL_MAX_DIST = 128

LANES = 128
V7X_VMEM_LIMIT_BYTES = 56 * 1024 * 1024

MASK_NEG = -1e30
M_INIT = -1e29
INT_MIN = int(np.iinfo(np.int32).min)
LOG2E = math.log2(math.e)
SB_UNDERFLOW_LOG2 = 160.0

NT_DIMS = (((1,), (1,)), ((), ()))


def _params(semantics):
    return pltpu.CompilerParams(dimension_semantics=semantics,
                                vmem_limit_bytes=V7X_VMEM_LIMIT_BYTES)


def _rms(x, g):
    return x * lax.rsqrt(jnp.mean(x * x, axis=-1, keepdims=True) + RMS_EPS) * g


def _normmod(x, g, shift, scale):
    return _rms(x, g * (1.0 + scale)) + shift


def _t5_bucket_thresholds(num_buckets):
    max_exact = num_buckets // 2

    def bucket(d):
        if d < max_exact:
            return d
        large = max_exact + int(math.log(max(d, 1) / max_exact)
                                / math.log(REL_MAX_DIST / max_exact) * (num_buckets - max_exact))
        return min(large, num_buckets - 1)

    thr, d = [], 0
    for b in range(num_buckets):
        while bucket(d) < b:
            d += 1
        thr.append(d)
    return tuple(thr)


def _ada_kernel(c_ref, w_ref, b_ref, o_ref):
    c = c_ref[...]
    s = c * jax.nn.sigmoid(c)
    o_ref[0] = jnp.dot(s.astype(BF16), w_ref[0].astype(BF16), preferred_element_type=F32) + b_ref[0]


def _ada_mod(c, ada_w, ada_b):
    L, D, E = ada_w.shape
    B = c.shape[0]
    tn = min(E, 1024)
    return pl.pallas_call(
        _ada_kernel,
        out_shape=jax.ShapeDtypeStruct((L, B, E), F32),
        grid=(L, E // tn),
        in_specs=[pl.BlockSpec((B, D), lambda l, j: (0, 0)),
                  pl.BlockSpec((1, D, tn), lambda l, j: (l, 0, j)),
                  pl.BlockSpec((1, 1, tn), lambda l, j: (l, 0, j))],
        out_specs=pl.BlockSpec((1, B, tn), lambda l, j: (l, 0, j)),
        compiler_params=_params(("parallel", "parallel")),
        name="ada_mod",
    )(c, ada_w, ada_b.reshape(L, 1, E))


def _in_proj_kernel(x_ref, g_ref, sh_ref, sc_ref, w_ref, o_ref, h_ref):
    @pl.when(pl.program_id(2) == 0)
    def _():
        h_ref[...] = _normmod(x_ref[0], g_ref[...], sh_ref[0], sc_ref[0]).astype(BF16)

    o_ref[0] = jnp.dot(h_ref[...], w_ref[...], preferred_element_type=F32).astype(o_ref.dtype)


def _in_proj(x, g, modr, shift_row, scale_row, w, out_dtype):
    B, S, D = x.shape
    n_out = w.shape[1]
    tm = min(S, 512)
    tn = min(n_out, 2048)
    return pl.pallas_call(
        _in_proj_kernel,
        out_shape=jax.ShapeDtypeStruct((B, S, n_out), out_dtype),
        grid=(B, S // tm, n_out // tn),
        in_specs=[pl.BlockSpec((1, tm, D), lambda b, i, j: (b, i, 0)),
                  pl.BlockSpec((1, D), lambda b, i, j: (0, 0)),
                  pl.BlockSpec((1, 1, D), lambda b, i, j: (shift_row(b), 0, 0)),
                  pl.BlockSpec((1, 1, D), lambda b, i, j: (scale_row(b), 0, 0)),
                  pl.BlockSpec((D, tn), lambda b, i, j: (0, j))],
        out_specs=pl.BlockSpec((1, tm, tn), lambda b, i, j: (b, i, j)),
        scratch_shapes=[pltpu.VMEM((tm, D), BF16)],
        compiler_params=_params(("parallel", "parallel", "arbitrary")),
        name="in_proj",
    )(x, g.reshape(1, D), modr, modr, w)


def _out_proj_kernel(a_ref, w_ref, x_ref, g_ref, gate_ref, o_ref):
    y = jnp.dot(a_ref[0], w_ref[...], preferred_element_type=F32)
    o_ref[0] = x_ref[0] + gate_ref[0] * _rms(y, g_ref[...])


def _out_proj(a, w, x, g, modr, gate_row):
    B, S, D = x.shape
    kin = a.shape[2]
    w_bytes = 2 * kin * D * w.dtype.itemsize
    tm = min(S, 512 if w_bytes <= V7X_VMEM_LIMIT_BYTES // 3 else 256)
    return pl.pallas_call(
        _out_proj_kernel,
        out_shape=jax.ShapeDtypeStruct((B, S, D), F32),
        grid=(B, S // tm),
        in_specs=[pl.BlockSpec((1, tm, kin), lambda b, i: (b, i, 0)),
                  pl.BlockSpec((kin, D), lambda b, i: (0, 0)),
                  pl.BlockSpec((1, tm, D), lambda b, i: (b, i, 0)),
                  pl.BlockSpec((1, D), lambda b, i: (0, 0)),
                  pl.BlockSpec((1, 1, D), lambda b, i: (gate_row(b), 0, 0))],
        out_specs=pl.BlockSpec((1, tm, D), lambda b, i: (b, i, 0)),
        compiler_params=_params(("parallel", "parallel")),
        name="out_proj",
    )(a, w, x, g.reshape(1, D), modr)


def _mlp_kernel(x_ref, g_in_ref, sh_ref, sc_ref, w1_ref, w2_ref, g_out_ref, gate_ref, o_ref, h_ref, acc_ref):
    f = pl.program_id(2)

    @pl.when(f == 0)
    def _():
        h_ref[...] = _normmod(x_ref[0], g_in_ref[...], sh_ref[0], sc_ref[0]).astype(BF16)
        acc_ref[...] = jnp.zeros_like(acc_ref)

    a = jnp.dot(h_ref[...], w1_ref[...], preferred_element_type=F32)
    a = jnp.square(jnp.maximum(a, 0.0)).astype(BF16)
    acc_ref[...] += jnp.dot(a, w2_ref[...], preferred_element_type=F32)

    @pl.when(f == pl.num_programs(2) - 1)
    def _():
        o_ref[0] = x_ref[0] + gate_ref[0] * _rms(acc_ref[...], g_out_ref[...])


def _mlp(x, g_in, g_out, modr, shift_row, scale_row, gate_row, w1, w2):
    B, S, D = x.shape
    dff = w1.shape[1]
    tm = min(S, 512)
    tf = min(dff, 1024)
    return pl.pallas_call(
        _mlp_kernel,
        out_shape=jax.ShapeDtypeStruct((B, S, D), F32),
        grid=(B, S // tm, dff // tf),
        in_specs=[pl.BlockSpec((1, tm, D), lambda b, i, f: (b, i, 0)),
                  pl.BlockSpec((1, D), lambda b, i, f: (0, 0)),
                  pl.BlockSpec((1, 1, D), lambda b, i, f: (shift_row(b), 0, 0)),
                  pl.BlockSpec((1, 1, D), lambda b, i, f: (scale_row(b), 0, 0)),
                  pl.BlockSpec((D, tf), lambda b, i, f: (0, f)),
                  pl.BlockSpec((tf, D), lambda b, i, f: (f, 0)),
                  pl.BlockSpec((1, D), lambda b, i, f: (0, 0)),
                  pl.BlockSpec((1, 1, D), lambda b, i, f: (gate_row(b), 0, 0))],
        out_specs=pl.BlockSpec((1, tm, D), lambda b, i, f: (b, i, 0)),
        scratch_shapes=[pltpu.VMEM((tm, D), BF16), pltpu.VMEM((tm, D), F32)],
        compiler_params=_params(("parallel", "parallel", "arbitrary")),
        name="mlp",
    )(x, g_in.reshape(1, D), modr, modr, w1, w2, g_out.reshape(1, D), modr)


def _dsa_proj_kernel(x_ref, g_ref, sh_ref, sc_ref, wq_ref, wkv_ref, wki_ref, wwi_ref,
                     qn_ref, kvn_ref, kin_ref, wqb_ref, wiqb_ref, wuk_ref,
                     qa_ref, qi_ref, ckv_ref, ki_ref, wi_ref, *, w_idx_scale):
    h = _normmod(x_ref[0], g_ref[...], sh_ref[0], sc_ref[0]).astype(BF16)
    q_lat = _rms(jnp.dot(h, wq_ref[...], preferred_element_type=F32), qn_ref[...]).astype(BF16)
    ckv_ref[0] = _rms(jnp.dot(h, wkv_ref[...], preferred_element_type=F32), kvn_ref[...]).astype(BF16)

    ki = jnp.dot(h, wki_ref[...], preferred_element_type=F32)
    mu = jnp.mean(ki, axis=-1, keepdims=True)
    var = jnp.mean(jnp.square(ki - mu), axis=-1, keepdims=True)
    ki_ref[0] = ((ki - mu) * lax.rsqrt(var + RMS_EPS) * kin_ref[...]).astype(BF16)

    n_ih = wi_ref.shape[2]
    wi = jnp.dot(h, wwi_ref[...], preferred_element_type=F32)
    wi_ref[0] = wi[:, :n_ih] * w_idx_scale

    n_heads, dh = wuk_ref.shape[0], wuk_ref.shape[1]
    q = jnp.dot(q_lat, wqb_ref[...], preferred_element_type=F32)
    for hd in range(n_heads):
        qh = q[:, hd * dh:(hd + 1) * dh].astype(BF16)
        qa_ref[0, hd] = jnp.dot(qh, wuk_ref[hd], preferred_element_type=F32).astype(BF16)

    di = qi_ref.shape[3]
    qidx = jnp.dot(q_lat, wiqb_ref[...], preferred_element_type=F32)
    for hd in range(qi_ref.shape[1]):
        qi_ref[0, hd] = qidx[:, hd * di:(hd + 1) * di].astype(BF16)


def _dsa_proj(x, g, modr, shift_row, scale_row, wq, wkv, wki, wwi, q_norm, kv_norm, idx_k_norm,
              w_qb, w_idx_qb, w_uk_t, n_idx_heads):
    B, S, D = x.shape
    rq, rkv, di = wq.shape[1], wkv.shape[1], wki.shape[1]
    n_heads, dh, _ = w_uk_t.shape
    tm = min(S, 256)
    const = lambda *shape: pl.BlockSpec(shape, lambda b, i: (0,) * len(shape))
    kern = functools.partial(_dsa_proj_kernel, w_idx_scale=float(n_idx_heads ** -0.5 * di ** -0.5))
    return pl.pallas_call(
        kern,
        out_shape=(jax.ShapeDtypeStruct((B, n_heads, S, rkv), BF16),
                   jax.ShapeDtypeStruct((B, n_idx_heads, S, di), BF16),
                   jax.ShapeDtypeStruct((B, S, rkv), BF16),
                   jax.ShapeDtypeStruct((B, S, di), BF16),
                   jax.ShapeDtypeStruct((B, S, n_idx_heads), F32)),
        grid=(B, S // tm),
        in_specs=[pl.BlockSpec((1, tm, D), lambda b, i: (b, i, 0)),
                  const(1, D),
                  pl.BlockSpec((1, 1, D), lambda b, i: (shift_row(b), 0, 0)),
                  pl.BlockSpec((1, 1, D), lambda b, i: (scale_row(b), 0, 0)),
                  const(D, rq), const(D, rkv), const(D, di), const(D, LANES),
                  const(1, rq), const(1, rkv), const(1, di),
                  const(rq, n_heads * dh), const(rq, n_idx_heads * di), const(n_heads, dh, rkv)],
        out_specs=(pl.BlockSpec((1, n_heads, tm, rkv), lambda b, i: (b, 0, i, 0)),
                   pl.BlockSpec((1, n_idx_heads, tm, di), lambda b, i: (b, 0, i, 0)),
                   pl.BlockSpec((1, tm, rkv), lambda b, i: (b, i, 0)),
                   pl.BlockSpec((1, tm, di), lambda b, i: (b, i, 0)),
                   pl.BlockSpec((1, tm, n_idx_heads), lambda b, i: (b, i, 0))),
        compiler_params=_params(("parallel", "parallel")),
        name="dsa_proj",
    )(x, g.reshape(1, D), modr, modr, wq, wkv, wki, wwi,
      q_norm.reshape(1, rq), kv_norm.reshape(1, rkv), idx_k_norm.reshape(1, di),
      w_qb, w_idx_qb, w_uk_t)


def _mono_key(score):
    score = jnp.where(score == 0.0, 0.0, score)
    bits = lax.bitcast_convert_type(score, jnp.int32)
    return bits ^ ((bits >> 31) & 0x7FFFFFFF)


def _indexer_kernel(qi_ref, ki_ref, w_ref, o_ref, key_ref, *, k_top, chunk):
    i = pl.program_id(1)
    n_ih, tq, di = qi_ref.shape[1:]
    n_pieces_total = o_ref.shape[2]
    per_chunk = chunk // LANES

    q2 = qi_ref[0].reshape(n_ih * tq, di)
    w = w_ref[0]
    wcols = [w[:, h:h + 1] for h in range(n_ih)]
    n_chunks = (i * tq + tq - 1) // chunk + 1
    row_t = i * tq + lax.broadcasted_iota(jnp.int32, (tq, chunk), 0)
    col_l = lax.broadcasted_iota(jnp.int32, (tq, chunk), 1)

    def chunk_body(c, carry):
        kc = ki_ref[0, pl.ds(pl.multiple_of(c * chunk, chunk), chunk), :]
        d = lax.dot_general(q2, kc, NT_DIMS, preferred_element_type=F32)
        sc = jnp.zeros((tq, chunk), F32)
        for h in range(n_ih):
            sc = sc + jnp.maximum(d[h * tq:(h + 1) * tq], 0.0) * wcols[h]
        key = jnp.where(c * chunk + col_l <= row_t, _mono_key(sc), INT_MIN)
        for k in range(per_chunk):
            key_ref[c * per_chunk + k] = key[:, k * LANES:(k + 1) * LANES]
        return carry

    lax.fori_loop(0, n_chunks, chunk_body, 0)
    n_pieces = n_chunks * per_chunk

    def count(pred):
        def body(c, cnt):
            for k in range(per_chunk):
                cnt = cnt + jnp.where(pred(key_ref[c * per_chunk + k]), 1.0, 0.0)
            return cnt
        cnt = lax.fori_loop(0, n_chunks, body, jnp.zeros((tq, LANES), F32))
        return jnp.broadcast_to(jnp.sum(cnt, axis=1, keepdims=True), (tq, LANES))

    kf = float(k_top)
    zero = jnp.zeros((tq, LANES), jnp.int32)
    prefix = jnp.where(count(lambda k: k >= zero) >= kf, zero, INT_MIN)

    def bit_body(b, prefix):
        cand = prefix | jnp.left_shift(jnp.int32(1), 30 - b)
        return jnp.where(count(lambda k: k >= cand) >= kf, cand, prefix)

    tau = lax.fori_loop(0, 31, bit_body, prefix)
    n_ge = count(lambda k: k >= tau)
    tied = jnp.max(n_ge) > kf

    def write(p, val):
        for r in range(tq // LANES):
            o_ref[0, r, p] = val[r * LANES:(r + 1) * LANES]

    @pl.when(jnp.logical_not(tied))
    def _():
        def body(p, carry):
            key = key_ref[p]
            write(p, jnp.where((key >= tau) & (key != INT_MIN), 0.0, MASK_NEG))
            return carry
        lax.fori_loop(0, n_pieces, body, 0)

    @pl.when(tied)
    def _():
        room = kf - count(lambda k: k > tau)
        before = (lax.broadcasted_iota(jnp.int32, (LANES, LANES), 0)
                  < lax.broadcasted_iota(jnp.int32, (LANES, LANES), 1)).astype(BF16)

        def body(p, seen):
            key = key_ref[p]
            tie = (key == tau) & (key != INT_MIN)
            tie_f = jnp.where(tie, 1.0, 0.0)
            rank = seen + jnp.dot(tie_f.astype(BF16), before, preferred_element_type=F32)
            write(p, jnp.where((key > tau) | (tie & (rank < room)), 0.0, MASK_NEG))
            return seen + jnp.broadcast_to(jnp.sum(tie_f, axis=1, keepdims=True), (tq, LANES))
        lax.fori_loop(0, n_pieces, body, jnp.zeros((tq, LANES), F32))

    def fill(p, carry):
        write(p, jnp.full((tq, LANES), MASK_NEG, F32))
        return carry
    lax.fori_loop(n_pieces, n_pieces_total, fill, 0)


def _indexer_mask(q_idx, k_idx, w_idx, k_top):
    B, n_ih, S, di = q_idx.shape
    tq = LANES
    chunk = min(S, 512)
    n_pieces = S // LANES
    kern = functools.partial(_indexer_kernel, k_top=k_top, chunk=chunk)
    return pl.pallas_call(
        kern,
        out_shape=jax.ShapeDtypeStruct((B, S // LANES, n_pieces, LANES, LANES), F32),
        grid=(B, S // tq),
        in_specs=[pl.BlockSpec((1, n_ih, tq, di), lambda b, i: (b, 0, i, 0)),
                  pl.BlockSpec((1, S, di), lambda b, i: (b, 0, 0)),
                  pl.BlockSpec((1, tq, n_ih), lambda b, i: (b, i, 0))],
        out_specs=pl.BlockSpec((1, tq // LANES, n_pieces, LANES, LANES), lambda b, i: (b, i, 0, 0, 0)),
        scratch_shapes=[pltpu.VMEM((n_pieces, tq, LANES), jnp.int32)],
        compiler_params=_params(("parallel", "parallel")),
        name="indexer_mask",
    )(q_idx, k_idx, w_idx)


def _bias_tile_kernel(rb_ref, o_ref, *, thresholds):
    kind = pl.program_id(0)
    h = pl.program_id(1)
    nb = len(thresholds)
    d = (lax.broadcasted_iota(jnp.int32, (LANES, LANES), 0)
         - lax.broadcasted_iota(jnp.int32, (LANES, LANES), 1) + kind * LANES)
    d = jnp.maximum(d, 0)
    far = rb_ref[nb - 1, h]
    val = jnp.full((LANES, LANES), (rb_ref[0, h] - far) * LOG2E, F32)
    for b in range(1, nb):
        val = jnp.where(d >= thresholds[b], (rb_ref[b, h] - far) * LOG2E, val)
    o_ref[0, 0] = jnp.where(kind == 2, 0.0, val)


def _bias_tiles(rel_bias):
    nb, n_heads = rel_bias.shape
    thresholds = _t5_bucket_thresholds(nb)
    assert thresholds[nb - 1] <= LANES + 1, "bias must saturate within one 128-key block"
    return pl.pallas_call(
        functools.partial(_bias_tile_kernel, thresholds=thresholds),
        out_shape=jax.ShapeDtypeStruct((3, n_heads, LANES, LANES), F32),
        grid=(3, n_heads),
        in_specs=[pl.BlockSpec(memory_space=pltpu.SMEM)],
        out_specs=pl.BlockSpec((1, 1, LANES, LANES), lambda k, h: (k, h, 0, 0)),
        compiler_params=_params(("parallel", "parallel")),
        name="bias_tiles",
    )(rel_bias)


def _dsa_attn_kernel(qa_ref, ckv_ref, mk_ref, tab_ref, wuv_ref, o_ref,
                     acc_ref, m_ref, l_ref, s0_ref, s1_ref, mb0_ref, mb1_ref, p_ref, a_ref, *, tk, unroll):
    i = pl.program_id(1)
    n_heads, tq, c = qa_ref.shape[1:]
    dv = wuv_ref.shape[2]
    npk = tk // LANES

    acc_ref[...] = jnp.zeros_like(acc_ref)
    m_ref[...] = jnp.full_like(m_ref, M_INIT)
    l_ref[...] = jnp.zeros_like(l_ref)

    def tile_kind(dist):
        return jnp.where(dist == 0, 0, jnp.where(dist == 1, 1, 2))

    def lanes(v, n):
        return jnp.concatenate([v] * (n // LANES), axis=1)

    n_groups = n_heads // unroll
    half = unroll // 2
    n_dots = unroll // half

    def split(n):
        return n // n_groups, pl.multiple_of((n % n_groups) * unroll, unroll)

    def scores(n, s_ref, mb_ref, with_bias):
        j, h0 = split(n)
        p0 = npk * j
        kv = ckv_ref[0, pl.ds(pl.multiple_of(j * tk, tk), tk), :]
        madd = jnp.concatenate([mk_ref[0, 0, p0 + k] for k in range(npk)], axis=1)
        for a in range(n_dots):
            q = qa_ref[0, pl.ds(h0 + a * half, half)].reshape(half * tq, c)
            sa = lax.dot_general(q, kv, NT_DIMS, preferred_element_type=F32)
            for u in range(a * half, (a + 1) * half):
                s = sa[(u - a * half) * tq:(u - a * half + 1) * tq] + madd
                if with_bias:
                    s = s + jnp.concatenate(
                        [tab_ref.at[tile_kind(i - (p0 + k)), pl.ds(h0, unroll)][u] for k in range(npk)], axis=1)
                s_ref[u * tq:(u + 1) * tq, :] = s
                mb_ref[u] = jnp.broadcast_to(jnp.max(s, axis=1, keepdims=True), (tq, LANES))

    def reduce(n, s_ref, mb_ref):
        j, h0 = split(n)
        acc_g = acc_ref.at[pl.ds(h0, unroll)]
        m_g, l_g = m_ref.at[pl.ds(h0, unroll)], l_ref.at[pl.ds(h0, unroll)]
        kv = ckv_ref[0, pl.ds(pl.multiple_of(j * tk, tk), tk), :]
        for u in range(unroll):
            m_old = m_g[u]
            m_new = jnp.maximum(m_old, mb_ref[u])
            alpha = jnp.exp2(m_old - m_new)
            p = jnp.exp2(s_ref[u * tq:(u + 1) * tq, :] - lanes(m_new, tk))
            l_g[u] = alpha * l_g[u] + jnp.sum(p, axis=1, keepdims=True)
            m_g[u] = m_new
            p_ref[u * tq:(u + 1) * tq, :] = p.astype(BF16)
            a_ref[u] = alpha
        for a in range(n_dots):
            pv = jnp.dot(p_ref[a * half * tq:(a + 1) * half * tq, :], kv, preferred_element_type=F32)
            for u in range(a * half, (a + 1) * half):
                acc_g[u] = lanes(a_ref[u], c) * acc_g[u] + pv[(u - a * half) * tq:(u - a * half + 1) * tq]

    def run(n_lo, n_hi, with_bias):
        @pl.when(n_hi > n_lo)
        def _():
            scores(n_lo, s0_ref, mb0_ref, with_bias)

            def pair(k, carry):
                n = n_lo + 2 * k
                scores(n + 1, s1_ref, mb1_ref, with_bias)
                reduce(n, s0_ref, mb0_ref)
                scores(jnp.minimum(n + 2, n_hi - 1), s0_ref, mb0_ref, with_bias)
                reduce(n + 1, s1_ref, mb1_ref)
                return carry

            lax.fori_loop(0, (n_hi - n_lo) // 2, pair, 0)

    n_kb = (i * tq + tq - 1) // tk + 1
    first_near = jnp.maximum((i - npk + 1) // npk, 0)
    run(0, first_near * n_groups, False)
    run(first_near * n_groups, n_kb * n_groups, True)

    for h in range(n_heads):
        o_lat = (acc_ref[h] * lanes(1.0 / l_ref[h], c)).astype(BF16)
        o_ref[0, :, h * dv:(h + 1) * dv] = jnp.dot(o_lat, wuv_ref[h], preferred_element_type=F32).astype(BF16)


def _dsa_attention(q_abs, c_kv, mask, bias_tab, w_uv):
    B, n_heads, S, c = q_abs.shape
    dv = w_uv.shape[2]
    tq = LANES
    tk = min(S, 2 * LANES)
    n_pieces = S // LANES
    unroll = min(n_heads, 8)
    assert (n_heads // unroll) % 2 == 0 and unroll % 2 == 0
    kern = functools.partial(_dsa_attn_kernel, tk=tk, unroll=unroll)
    return pl.pallas_call(
        kern,
        out_shape=jax.ShapeDtypeStruct((B, S, n_heads * dv), BF16),
        grid=(B, S // tq),
        in_specs=[pl.BlockSpec((1, n_heads, tq, c), lambda b, i: (b, 0, i, 0)),
                  pl.BlockSpec((1, S, c), lambda b, i: (b, 0, 0)),
                  pl.BlockSpec((1, 1, n_pieces, tq, LANES), lambda b, i: (b, i, 0, 0, 0)),
                  pl.BlockSpec((3, n_heads, LANES, LANES), lambda b, i: (0, 0, 0, 0)),
                  pl.BlockSpec((n_heads, c, dv), lambda b, i: (0, 0, 0))],
        out_specs=pl.BlockSpec((1, tq, n_heads * dv), lambda b, i: (b, i, 0)),
        scratch_shapes=[pltpu.VMEM((n_heads, tq, c), F32),
                        pltpu.VMEM((n_heads, tq, LANES), F32),
                        pltpu.VMEM((n_heads, tq, LANES), F32),
                        pltpu.VMEM((unroll * tq, tk), F32),
                        pltpu.VMEM((unroll * tq, tk), F32),
                        pltpu.VMEM((unroll, tq, LANES), F32),
                        pltpu.VMEM((unroll, tq, LANES), F32),
                        pltpu.VMEM((unroll * tq, tk), BF16),
                        pltpu.VMEM((unroll, tq, LANES), F32)],
        compiler_params=_params(("parallel", "parallel")),
        name="dsa_attention",
    )(q_abs, c_kv, mask, bias_tab, w_uv)


def _sb_attn_kernel(q_ref, k_ref, v_ref, o_ref, acc_ref, rest_ref, *, dh):
    i = pl.program_id(2)
    tq = q_ref.shape[1]
    tk = tq
    nh = q_ref.shape[2] // dh
    below = (lax.broadcasted_iota(jnp.int32, (tq, tk), 1)
             < lax.broadcasted_iota(jnp.int32, (tq, tk), 0))
    from_s = (lax.broadcasted_iota(jnp.int32, (2 * tk, tk), 0) % tk
              >= lax.broadcasted_iota(jnp.int32, (2 * tk, tk), 1)).astype(BF16)

    acc_ref[...] = jnp.zeros_like(acc_ref)
    rest_ref[...] = jnp.zeros_like(rest_ref)


    def block(j, diag):
        k0 = pl.multiple_of(j * tk, tk)
        z2s, log_rests, parts = [], [], []
        for u in range(nh):
            hd = slice(u * dh, (u + 1) * dh)
            z2 = lax.dot_general(q_ref[0, :, hd], k_ref[0, pl.ds(k0, tk), hd], NT_DIMS,
                                 preferred_element_type=F32)
            neg_abs = lax.bitcast_convert_type(
                lax.bitcast_convert_type(z2, jnp.uint32) | jnp.uint32(0x80000000), F32)
            log_rest = jnp.log(1.0 + jnp.exp2(neg_abs)) * (-LOG2E) - jnp.maximum(z2, 0.0)
            if diag:
                log_rest = jnp.where(below, log_rest, 0.0)
            hi = lax.bitcast_convert_type(
                lax.bitcast_convert_type(log_rest, jnp.uint32) & jnp.uint32(0xFFFF0000), F32)
            parts.append(jnp.concatenate([hi.astype(BF16), (log_rest - hi).astype(BF16)], axis=1))
            z2s.append(z2)
            log_rests.append(log_rest)
        cums = [jnp.dot(part, from_s, preferred_element_type=F32) for part in parts]
        for u in range(nh):
            hd = slice(u * dh, (u + 1) * dh)
            rest = rest_ref[u]
            wgt = jnp.exp2(z2s[u] + cums[u] + jnp.concatenate([rest] * (tk // LANES), axis=1))
            if diag:
                wgt = jnp.where(below, wgt, 0.0)
            acc_ref[:, hd] += jnp.dot(wgt.astype(BF16), v_ref[0, pl.ds(k0, tk), hd], preferred_element_type=F32)
            rest_ref[u] = rest + jnp.sum(log_rests[u], axis=1, keepdims=True)

    def all_underflowed():
        worst = jnp.max(jnp.max(rest_ref[...], axis=0))
        return (worst <= -SB_UNDERFLOW_LOG2).astype(jnp.int32)

    block(i, True)

    def more(carry):
        jj, done = carry
        return jnp.logical_and(jj <= i, done == 0)

    def step(carry):
        jj, _ = carry
        block(i - jj, False)
        return jj + 1, all_underflowed()

    lax.while_loop(more, step, (jnp.int32(1), all_underflowed()))
    o_ref[0] = acc_ref[...].astype(o_ref.dtype)


def _sb_attention(qkv, n_heads):
    B, S, three_d = qkv.shape
    dh = three_d // (3 * n_heads)
    tq = min(S, 2 * LANES)
    nh = min(n_heads, 4)
    n_groups = n_heads // nh
    kern = functools.partial(_sb_attn_kernel, dh=dh)
    return pl.pallas_call(
        kern,
        out_shape=jax.ShapeDtypeStruct((B, S, n_heads * dh), BF16),
        grid=(B, n_groups, S // tq),
        in_specs=[pl.BlockSpec((1, tq, nh * dh), lambda b, g, i: (b, i, g)),
                  pl.BlockSpec((1, S, nh * dh), lambda b, g, i: (b, 0, n_groups + g)),
                  pl.BlockSpec((1, S, nh * dh), lambda b, g, i: (b, 0, 2 * n_groups + g))],
        out_specs=pl.BlockSpec((1, tq, nh * dh), lambda b, g, i: (b, i, g)),
        scratch_shapes=[pltpu.VMEM((tq, nh * dh), F32), pltpu.VMEM((nh, tq, LANES), F32)],
        compiler_params=_params(("parallel", "parallel", "parallel")),
        name="sb_attention",
    )(qkv, qkv, qkv)


def kernel(x, c, rel_bias, ada_w, ada_b, norm_g, a_w_in, a_q_norm, a_kv_norm, a_idx_k_norm, a_w_qb, a_w_idx_qb,
           a_w_uk, a_w_uv, a_w_out, b_w_in, b_w_out, mlp_w1, mlp_w2):
    B, S, D = x.shape
    depth = ada_w.shape[0]
    n_mixers = 2

    mod = _ada_mod(c, ada_w, ada_b)
    modr = mod.reshape(depth * B * 6, 1, D)

    def mod_row(layer, chunk):
        return lambda b: (layer * B + b) * 6 + chunk

    rq, rkv, di = a_q_norm.shape[1], a_kv_norm.shape[1], a_idx_k_norm.shape[1]
    n_a_heads, _, a_qk_dim = a_w_uk.shape[1:]
    n_idx_heads = a_w_idx_qb.shape[2] // di
    k_top = min(IDX_TOPK, S // 4)
    bias_tab = _bias_tiles(rel_bias)

    for layer in range(depth):
        j = layer // n_mixers
        g = norm_g[layer]
        if layer % n_mixers == 0:
            w_in = a_w_in[j].astype(BF16)
            wwi = jnp.pad(w_in[:, rq + rkv + di:], ((0, 0), (0, LANES - n_idx_heads)))
            q_abs, q_idx, c_kv, k_idx, w_idx = _dsa_proj(
                x, g[0], modr, mod_row(layer, 0), mod_row(layer, 1),
                w_in[:, :rq], w_in[:, rq:rq + rkv], w_in[:, rq + rkv:rq + rkv + di], wwi,
                a_q_norm[j], a_kv_norm[j], a_idx_k_norm[j],
                a_w_qb[j].astype(BF16), a_w_idx_qb[j].astype(BF16),
                (jnp.swapaxes(a_w_uk[j], 1, 2) * (a_qk_dim ** -0.5 * LOG2E)).astype(BF16), n_idx_heads)
            mask = _indexer_mask(q_idx, k_idx, w_idx, k_top)
            y = _dsa_attention(q_abs, c_kv, mask, bias_tab, a_w_uv[j].astype(BF16))
            w_out = a_w_out[j].astype(BF16)
        else:
            d_sb = b_w_in.shape[2] // 3
            col_scale = jnp.where(jnp.arange(3 * d_sb) < d_sb, (d_sb // B_HEADS) ** -0.5 * LOG2E, 1.0)
            qkv = _in_proj(x, g[0], modr, mod_row(layer, 0), mod_row(layer, 1),
                           (b_w_in[j] * col_scale).astype(BF16), BF16)
            y = _sb_attention(qkv, B_HEADS)
            w_out = b_w_out[j].astype(BF16)
        x = _out_proj(y, w_out, x, g[1], modr, mod_row(layer, 2))
        x = _mlp(x, g[2], g[3], modr, mod_row(layer, 3), mod_row(layer, 4), mod_row(layer, 5),
                 mlp_w1[layer].astype(BF16), mlp_w2[layer].astype(BF16))
    return x
```

```python
import functools
import math

import numpy as np
import jax
import jax.numpy as jnp
from jax import lax
from jax.experimental import pallas as pl
from jax.experimental.pallas import tpu as pltpu

F32 = jnp.float32
BF16 = jnp.bfloat16

RMS_EPS = 1e-6
B_HEADS = 16
IDX_TOPK = 256
REL_MAX_DIST = 128

LANES = 128
V7X_VMEM_LIMIT_BYTES = 56 * 1024 * 1024

MASK_NEG = -1e30
M_INIT = -1e29
INT_MIN = int(np.iinfo(np.int32).min)
LOG2E = math.log2(math.e)
SB_UNDERFLOW_LOG2 = 160.0

NT_DIMS = (((1,), (1,)), ((), ()))


def _params(semantics):
    return pltpu.CompilerParams(dimension_semantics=semantics,
                                vmem_limit_bytes=V7X_VMEM_LIMIT_BYTES)


def _rms(x, g):
    return x * lax.rsqrt(jnp.mean(x * x, axis=-1, keepdims=True) + RMS_EPS) * g


def _normmod(x, g, shift, scale):
    return _rms(x, g * (1.0 + scale)) + shift


def _t5_bucket_thresholds(num_buckets):
    max_exact = num_buckets // 2

    def bucket(d):
        if d < max_exact:
            return d
        large = max_exact + int(math.log(max(d, 1) / max_exact)
                                / math.log(REL_MAX_DIST / max_exact) * (num_buckets - max_exact))
        return min(large, num_buckets - 1)

    thr, d = [], 0
    for b in range(num_buckets):
        while bucket(d) < b:
            d += 1
        thr.append(d)
    return tuple(thr)


def _ada_kernel(c_ref, w_ref, b_ref, o_ref):
    c = c_ref[...]
    s = c * jax.nn.sigmoid(c)
    o_ref[0] = jnp.dot(s.astype(BF16), w_ref[0].astype(BF16), preferred_element_type=F32) + b_ref[0]


def _ada_mod(c, ada_w, ada_b):
    L, D, E = ada_w.shape
    B = c.shape[0]
    tn = min(E, 1024)
    return pl.pallas_call(
        _ada_kernel,
        out_shape=jax.ShapeDtypeStruct((L, B, E), F32),
        grid=(L, E // tn),
        in_specs=[pl.BlockSpec((B, D), lambda l, j: (0, 0)),
                  pl.BlockSpec((1, D, tn), lambda l, j: (l, 0, j)),
                  pl.BlockSpec((1, 1, tn), lambda l, j: (l, 0, j))],
        out_specs=pl.BlockSpec((1, B, tn), lambda l, j: (l, 0, j)),
        compiler_params=_params(("parallel", "parallel")),
        name="ada_mod",
    )(c, ada_w, ada_b.reshape(L, 1, E))


def _in_proj_kernel(x_ref, g_ref, sh_ref, sc_ref, w_ref, o_ref, h_ref):
    @pl.when(pl.program_id(2) == 0)
    def _():
        h_ref[...] = _normmod(x_ref[0], g_ref[...], sh_ref[0], sc_ref[0]).astype(BF16)

    o_ref[0] = jnp.dot(h_ref[...], w_ref[...], preferred_element_type=F32).astype(o_ref.dtype)


def _in_proj(x, g, modr, shift_row, scale_row, w, out_dtype):
    B, S, D = x.shape
    n_out = w.shape[1]
    tm = min(S, 512)
    tn = min(n_out, 2048)
    return pl.pallas_call(
        _in_proj_kernel,
        out_shape=jax.ShapeDtypeStruct((B, S, n_out), out_dtype),
        grid=(B, S // tm, n_out // tn),
        in_specs=[pl.BlockSpec((1, tm, D), lambda b, i, j: (b, i, 0)),
                  pl.BlockSpec((1, D), lambda b, i, j: (0, 0)),
                  pl.BlockSpec((1, 1, D), lambda b, i, j: (shift_row(b), 0, 0)),
                  pl.BlockSpec((1, 1, D), lambda b, i, j: (scale_row(b), 0, 0)),
                  pl.BlockSpec((D, tn), lambda b, i, j: (0, j))],
        out_specs=pl.BlockSpec((1, tm, tn), lambda b, i, j: (b, i, j)),
        scratch_shapes=[pltpu.VMEM((tm, D), BF16)],
        compiler_params=_params(("parallel", "parallel", "arbitrary")),
        name="in_proj",
    )(x, g.reshape(1, D), modr, modr, w)


def _out_proj_kernel(a_ref, w_ref, x_ref, g_ref, gate_ref, o_ref):
    y = jnp.dot(a_ref[0], w_ref[...], preferred_element_type=F32)
    o_ref[0] = x_ref[0] + gate_ref[0] * _rms(y, g_ref[...])


def _out_proj(a, w, x, g, modr, gate_row):
    B, S, D = x.shape
    kin = a.shape[2]
    w_bytes = 2 * kin * D * w.dtype.itemsize
    tm = min(S, 512 if w_bytes <= V7X_VMEM_LIMIT_BYTES // 3 else 256)
    return pl.pallas_call(
        _out_proj_kernel,
        out_shape=jax.ShapeDtypeStruct((B, S, D), F32),
        grid=(B, S // tm),
        in_specs=[pl.BlockSpec((1, tm, kin), lambda b, i: (b, i, 0)),
                  pl.BlockSpec((kin, D), lambda b, i: (0, 0)),
                  pl.BlockSpec((1, tm, D), lambda b, i: (b, i, 0)),
                  pl.BlockSpec((1, D), lambda b, i: (0, 0)),
                  pl.BlockSpec((1, 1, D), lambda b, i: (gate_row(b), 0, 0))],
        out_specs=pl.BlockSpec((1, tm, D), lambda b, i: (b, i, 0)),
        compiler_params=_params(("parallel", "parallel")),
        name="out_proj",
    )(a, w, x, g.reshape(1, D), modr)


def _mlp_kernel(x_ref, g_in_ref, sh_ref, sc_ref, w1_ref, w2_ref, g_out_ref, gate_ref, o_ref, h_ref, acc_ref):
    f = pl.program_id(2)

    @pl.when(f == 0)
    def _():
        h_ref[...] = _normmod(x_ref[0], g_in_ref[...], sh_ref[0], sc_ref[0]).astype(BF16)
        acc_ref[...] = jnp.zeros_like(acc_ref)

    a = jnp.dot(h_ref[...], w1_ref[...], preferred_element_type=F32)
    a = jnp.square(jnp.maximum(a, 0.0)).astype(BF16)
    acc_ref[...] += jnp.dot(a, w2_ref[...], preferred_element_type=F32)

    @pl.when(f == pl.num_programs(2) - 1)
    def _():
        o_ref[0] = x_ref[0] + gate_ref[0] * _rms(acc_ref[...], g_out_ref[...])


def _mlp(x, g_in, g_out, modr, shift_row, scale_row, gate_row, w1, w2):
    B, S, D = x.shape
    dff = w1.shape[1]
    tm = min(S, 512)
    tf = min(dff, 1024)
    return pl.pallas_call(
        _mlp_kernel,
        out_shape=jax.ShapeDtypeStruct((B, S, D), F32),
        grid=(B, S // tm, dff // tf),
        in_specs=[pl.BlockSpec((1, tm, D), lambda b, i, f: (b, i, 0)),
                  pl.BlockSpec((1, D), lambda b, i, f: (0, 0)),
                  pl.BlockSpec((1, 1, D), lambda b, i, f: (shift_row(b), 0, 0)),
                  pl.BlockSpec((1, 1, D), lambda b, i, f: (scale_row(b), 0, 0)),
                  pl.BlockSpec((D, tf), lambda b, i, f: (0, f)),
                  pl.BlockSpec((tf, D), lambda b, i, f: (f, 0)),
                  pl.BlockSpec((1, D), lambda b, i, f: (0, 0)),
                  pl.BlockSpec((1, 1, D), lambda b, i, f: (gate_row(b), 0, 0))],
        out_specs=pl.BlockSpec((1, tm, D), lambda b, i, f: (b, i, 0)),
        scratch_shapes=[pltpu.VMEM((tm, D), BF16), pltpu.VMEM((tm, D), F32)],
        compiler_params=_params(("parallel", "parallel", "arbitrary")),
        name="mlp",
    )(x, g_in.reshape(1, D), modr, modr, w1, w2, g_out.reshape(1, D), modr)


def _dsa_proj_kernel(x_ref, g_ref, sh_ref, sc_ref, wq_ref, wkv_ref, wki_ref, wwi_ref,
                     qn_ref, kvn_ref, kin_ref, wqb_ref, wiqb_ref, wuk_ref,
                     qa_ref, qi_ref, ckv_ref, ki_ref, wi_ref, *, w_idx_scale):
    h = _normmod(x_ref[0], g_ref[...], sh_ref[0], sc_ref[0]).astype(BF16)
    q_lat = _rms(jnp.dot(h, wq_ref[...], preferred_element_type=F32), qn_ref[...]).astype(BF16)
    ckv_ref[0] = _rms(jnp.dot(h, wkv_ref[...], preferred_element_type=F32), kvn_ref[...]).astype(BF16)

    ki = jnp.dot(h, wki_ref[...], preferred_element_type=F32)
    mu = jnp.mean(ki, axis=-1, keepdims=True)
    var = jnp.mean(jnp.square(ki - mu), axis=-1, keepdims=True)
    ki_ref[0] = ((ki - mu) * lax.rsqrt(var + RMS_EPS) * kin_ref[...]).astype(BF16)

    n_ih = wi_ref.shape[2]
    wi = jnp.dot(h, wwi_ref[...], preferred_element_type=F32)
    wi_ref[0] = wi[:, :n_ih] * w_idx_scale

    n_heads, dh = wuk_ref.shape[0], wuk_ref.shape[1]
    q = jnp.dot(q_lat, wqb_ref[...], preferred_element_type=F32)
    for hd in range(n_heads):
        qh = q[:, hd * dh:(hd + 1) * dh].astype(BF16)
        qa_ref[0, hd] = jnp.dot(qh, wuk_ref[hd], preferred_element_type=F32).astype(BF16)

    di = qi_ref.shape[3]
    qidx = jnp.dot(q_lat, wiqb_ref[...], preferred_element_type=F32)
    for hd in range(qi_ref.shape[1]):
        qi_ref[0, hd] = qidx[:, hd * di:(hd + 1) * di].astype(BF16)


def _dsa_proj(x, g, modr, shift_row, scale_row, wq, wkv, wki, wwi, q_norm, kv_norm, idx_k_norm,
              w_qb, w_idx_qb, w_uk_t, n_idx_heads):
    B, S, D = x.shape
    rq, rkv, di = wq.shape[1], wkv.shape[1], wki.shape[1]
    n_heads, dh, _ = w_uk_t.shape
    tm = min(S, 256)
    const = lambda *shape: pl.BlockSpec(shape, lambda b, i: (0,) * len(shape))
    kern = functools.partial(_dsa_proj_kernel, w_idx_scale=float(n_idx_heads ** -0.5 * di ** -0.5))
    return pl.pallas_call(
        kern,
        out_shape=(jax.ShapeDtypeStruct((B, n_heads, S, rkv), BF16),
                   jax.ShapeDtypeStruct((B, n_idx_heads, S, di), BF16),
                   jax.ShapeDtypeStruct((B, S, rkv), BF16),
                   jax.ShapeDtypeStruct((B, S, di), BF16),
                   jax.ShapeDtypeStruct((B, S, n_idx_heads), F32)),
        grid=(B, S // tm),
        in_specs=[pl.BlockSpec((1, tm, D), lambda b, i: (b, i, 0)),
                  const(1, D),
                  pl.BlockSpec((1, 1, D), lambda b, i: (shift_row(b), 0, 0)),
                  pl.BlockSpec((1, 1, D), lambda b, i: (scale_row(b), 0, 0)),
                  const(D, rq), const(D, rkv), const(D, di), const(D, LANES),
                  const(1, rq), const(1, rkv), const(1, di),
                  const(rq, n_heads * dh), const(rq, n_idx_heads * di), const(n_heads, dh, rkv)],
        out_specs=(pl.BlockSpec((1, n_heads, tm, rkv), lambda b, i: (b, 0, i, 0)),
                   pl.BlockSpec((1, n_idx_heads, tm, di), lambda b, i: (b, 0, i, 0)),
                   pl.BlockSpec((1, tm, rkv), lambda b, i: (b, i, 0)),
                   pl.BlockSpec((1, tm, di), lambda b, i: (b, i, 0)),
                   pl.BlockSpec((1, tm, n_idx_heads), lambda b, i: (b, i, 0))),
        compiler_params=_params(("parallel", "parallel")),
        name="dsa_proj",
    )(x, g.reshape(1, D), modr, modr, wq, wkv, wki, wwi,
      q_norm.reshape(1, rq), kv_norm.reshape(1, rkv), idx_k_norm.reshape(1, di),
      w_qb, w_idx_qb, w_uk_t)


def _mono_key(score):
    score = jnp.where(score == 0.0, 0.0, score)
    bits = lax.bitcast_convert_type(score, jnp.int32)
    return bits ^ ((bits >> 31) & 0x7FFFFFFF)


def _indexer_kernel(qi_ref, ki_ref, w_ref, o_ref, key_ref, *, k_top, chunk):
    i = pl.program_id(1)
    n_ih, tq, di = qi_ref.shape[1:]
    n_pieces_total = o_ref.shape[2]
    per_chunk = chunk // LANES

    q2 = qi_ref[0].reshape(n_ih * tq, di)
    w = w_ref[0]
    wcols = [w[:, h:h + 1] for h in range(n_ih)]
    n_chunks = (i * tq + tq - 1) // chunk + 1
    row_t = i * tq + lax.broadcasted_iota(jnp.int32, (tq, chunk), 0)
    col_l = lax.broadcasted_iota(jnp.int32, (tq, chunk), 1)

    def chunk_body(c, carry):
        kc = ki_ref[0, pl.ds(pl.multiple_of(c * chunk, chunk), chunk), :]
        d = lax.dot_general(q2, kc, NT_DIMS, preferred_element_type=F32)
        sc = jnp.zeros((tq, chunk), F32)
        for h in range(n_ih):
            sc = sc + jnp.maximum(d[h * tq:(h + 1) * tq], 0.0) * wcols[h]
        key = jnp.where(c * chunk + col_l <= row_t, _mono_key(sc), INT_MIN)
        for k in range(per_chunk):
            key_ref[c * per_chunk + k] = key[:, k * LANES:(k + 1) * LANES]
        return carry

    lax.fori_loop(0, n_chunks, chunk_body, 0)
    n_pieces = n_chunks * per_chunk

    def count(pred):
        def body(c, cnt):
            for k in range(per_chunk):
                cnt = cnt + jnp.where(pred(key_ref[c * per_chunk + k]), 1.0, 0.0)
            return cnt
        cnt = lax.fori_loop(0, n_chunks, body, jnp.zeros((tq, LANES), F32))
        return jnp.broadcast_to(jnp.sum(cnt, axis=1, keepdims=True), (tq, LANES))

    kf = float(k_top)
    zero = jnp.zeros((tq, LANES), jnp.int32)
    prefix = jnp.where(count(lambda k: k >= zero) >= kf, zero, INT_MIN)

    def bit_body(b, prefix):
        cand = prefix | jnp.left_shift(jnp.int32(1), 30 - b)
        return jnp.where(count(lambda k: k >= cand) >= kf, cand, prefix)

    tau = lax.fori_loop(0, 31, bit_body, prefix)
    n_ge = count(lambda k: k >= tau)
    tied = jnp.max(n_ge) > kf

    def write(p, val):
        for r in range(tq // LANES):
            o_ref[0, r, p] = val[r * LANES:(r + 1) * LANES]

    @pl.when(jnp.logical_not(tied))
    def _():
        def body(p, carry):
            key = key_ref[p]
            write(p, jnp.where((key >= tau) & (key != INT_MIN), 0.0, MASK_NEG))
            return carry
        lax.fori_loop(0, n_pieces, body, 0)

    @pl.when(tied)
    def _():
        room = kf - count(lambda k: k > tau)
        before = (lax.broadcasted_iota(jnp.int32, (LANES, LANES), 0)
                  < lax.broadcasted_iota(jnp.int32, (LANES, LANES), 1)).astype(BF16)

        def body(p, seen):
            key = key_ref[p]
            tie = (key == tau) & (key != INT_MIN)
            tie_f = jnp.where(tie, 1.0, 0.0)
            rank = seen + jnp.dot(tie_f.astype(BF16), before, preferred_element_type=F32)
            write(p, jnp.where((key > tau) | (tie & (rank < room)), 0.0, MASK_NEG))
            return seen + jnp.broadcast_to(jnp.sum(tie_f, axis=1, keepdims=True), (tq, LANES))
        lax.fori_loop(0, n_pieces, body, jnp.zeros((tq, LANES), F32))

    def fill(p, carry):
        write(p, jnp.full((tq, LANES), MASK_NEG, F32))
        return carry
    lax.fori_loop(n_pieces, n_pieces_total, fill, 0)


def _indexer_mask(q_idx, k_idx, w_idx, k_top):
    B, n_ih, S, di = q_idx.shape
    tq = LANES
    chunk = min(S, 512)
    n_pieces = S // LANES
    kern = functools.partial(_indexer_kernel, k_top=k_top, chunk=chunk)
    return pl.pallas_call(
        kern,
        out_shape=jax.ShapeDtypeStruct((B, S // LANES, n_pieces, LANES, LANES), F32),
        grid=(B, S // tq),
        in_specs=[pl.BlockSpec((1, n_ih, tq, di), lambda b, i: (b, 0, i, 0)),
                  pl.BlockSpec((1, S, di), lambda b, i: (b, 0, 0)),
                  pl.BlockSpec((1, tq, n_ih), lambda b, i: (b, i, 0))],
        out_specs=pl.BlockSpec((1, tq // LANES, n_pieces, LANES, LANES), lambda b, i: (b, i, 0, 0, 0)),
        scratch_shapes=[pltpu.VMEM((n_pieces, tq, LANES), jnp.int32)],
        compiler_params=_params(("parallel", "parallel")),
        name="indexer_mask",
    )(q_idx, k_idx, w_idx)


def _bias_tile_kernel(rb_ref, o_ref, *, thresholds):
    kind = pl.program_id(0)
    h = pl.program_id(1)
    nb = len(thresholds)
    d = (lax.broadcasted_iota(jnp.int32, (LANES, LANES), 0)
         - lax.broadcasted_iota(jnp.int32, (LANES, LANES), 1) + kind * LANES)
    d = jnp.maximum(d, 0)
    far = rb_ref[nb - 1, h]
    val = jnp.full((LANES, LANES), (rb_ref[0, h] - far) * LOG2E, F32)
    for b in range(1, nb):
        val = jnp.where(d >= thresholds[b], (rb_ref[b, h] - far) * LOG2E, val)
    o_ref[0, 0] = jnp.where(kind == 2, 0.0, val)


def _bias_tiles(rel_bias):
    nb, n_heads = rel_bias.shape
    thresholds = _t5_bucket_thresholds(nb)
    assert thresholds[nb - 1] <= LANES + 1, "bias must saturate within one 128-key block"
    return pl.pallas_call(
        functools.partial(_bias_tile_kernel, thresholds=thresholds),
        out_shape=jax.ShapeDtypeStruct((3, n_heads, LANES, LANES), F32),
        grid=(3, n_heads),
        in_specs=[pl.BlockSpec(memory_space=pltpu.SMEM)],
        out_specs=pl.BlockSpec((1, 1, LANES, LANES), lambda k, h: (k, h, 0, 0)),
        compiler_params=_params(("parallel", "parallel")),
        name="bias_tiles",
    )(rel_bias)


def _dsa_attn_kernel(qa_ref, ckv_ref, mk_ref, tab_ref, wuv_ref, o_ref,
                     acc_ref, m_ref, l_ref, s0_ref, s1_ref, mb0_ref, mb1_ref, p_ref, a_ref, *, tk, unroll):
    i = pl.program_id(1)
    n_heads, tq, c = qa_ref.shape[1:]
    dv = wuv_ref.shape[2]
    npk = tk // LANES

    def tile_kind(dist):
        return jnp.where(dist == 0, 0, jnp.where(dist == 1, 1, 2))

    def lanes(v, n):
        return jnp.concatenate([v] * (n // LANES), axis=1)

    n_groups = n_heads // unroll
    half = unroll // 2
    n_dots = unroll // half

    def split(n):
        return n // n_groups, pl.multiple_of((n % n_groups) * unroll, unroll)

    def scores(n, s_ref, mb_ref, with_bias):
        j, h0 = split(n)
        p0 = npk * j
        kv = ckv_ref[0, pl.ds(pl.multiple_of(j * tk, tk), tk), :]
        madd = jnp.concatenate([mk_ref[0, 0, p0 + k] for k in range(npk)], axis=1)
        for a in range(n_dots):
            q = qa_ref[0, pl.ds(h0 + a * half, half)].reshape(half * tq, c)
            sa = lax.dot_general(q, kv, NT_DIMS, preferred_element_type=F32)
            for u in range(a * half, (a + 1) * half):
                s = sa[(u - a * half) * tq:(u - a * half + 1) * tq] + madd
                if with_bias:
                    s = s + jnp.concatenate(
                        [tab_ref.at[tile_kind(i - (p0 + k)), pl.ds(h0, unroll)][u] for k in range(npk)], axis=1)
                s_ref[u * tq:(u + 1) * tq, :] = s
                mb_ref[u] = jnp.broadcast_to(jnp.max(s, axis=1, keepdims=True), (tq, LANES))

    def reduce(n, s_ref, mb_ref, first):
        j, h0 = split(n)
        acc_g = acc_ref.at[pl.ds(h0, unroll)]
        m_g, l_g = m_ref.at[pl.ds(h0, unroll)], l_ref.at[pl.ds(h0, unroll)]
        kv = ckv_ref[0, pl.ds(pl.multiple_of(j * tk, tk), tk), :]
        for u in range(unroll):
            m_old = M_INIT if first else m_g[u]
            m_new = jnp.maximum(mb_ref[u], m_old)
            p = jnp.exp2(s_ref[u * tq:(u + 1) * tq, :] - lanes(m_new, tk))
            row_sum = jnp.sum(p, axis=1, keepdims=True)
            if first:
                l_g[u] = jnp.broadcast_to(row_sum, (tq, LANES))
            else:
                alpha = jnp.exp2(m_old - m_new)
                l_g[u] = alpha * l_g[u] + row_sum
                a_ref[u] = alpha
            m_g[u] = m_new
            p_ref[u * tq:(u + 1) * tq, :] = p.astype(BF16)
        for a in range(n_dots):
            pv = jnp.dot(p_ref[a * half * tq:(a + 1) * half * tq, :], kv, preferred_element_type=F32)
            for u in range(a * half, (a + 1) * half):
                pv_u = pv[(u - a * half) * tq:(u - a * half + 1) * tq]
                acc_g[u] = pv_u if first else lanes(a_ref[u], c) * acc_g[u] + pv_u

    def run(n_lo, n_hi, with_bias, first=False):
        @pl.when(n_hi > n_lo)
        def _():
            scores(n_lo, s0_ref, mb0_ref, with_bias)

            def pair(k, carry):
                n = n_lo + 2 * k
                scores(n + 1, s1_ref, mb1_ref, with_bias)
                reduce(n, s0_ref, mb0_ref, first)
                scores(jnp.minimum(n + 2, n_hi - 1), s0_ref, mb0_ref, with_bias)
                reduce(n + 1, s1_ref, mb1_ref, first)
                return carry

            lax.fori_loop(0, (n_hi - n_lo) // 2, pair, 0)

    n_kb = (i * tq + tq - 1) // tk + 1
    first_near = jnp.maximum((i - npk + 1) // npk, 0)

    @pl.when(first_near == 0)
    def _():
        run(0, n_groups, True, first=True)

    @pl.when(first_near != 0)
    def _():
        run(0, n_groups, False, first=True)

    run(n_groups, first_near * n_groups, False)
    run(jnp.maximum(first_near, 1) * n_groups, n_kb * n_groups, True)

    for h in range(n_heads):
        o_lat = (acc_ref[h] * lanes(1.0 / l_ref[h], c)).astype(BF16)
        o_ref[0, :, h * dv:(h + 1) * dv] = jnp.dot(o_lat, wuv_ref[h], preferred_element_type=F32).astype(BF16)


def _dsa_attention(q_abs, c_kv, mask, bias_tab, w_uv):
    B, n_heads, S, c = q_abs.shape
    dv = w_uv.shape[2]
    tq = LANES
    tk = min(S, 2 * LANES)
    n_pieces = S // LANES
    unroll = min(n_heads, 8)
    assert (n_heads // unroll) % 2 == 0 and unroll % 2 == 0
    kern = functools.partial(_dsa_attn_kernel, tk=tk, unroll=unroll)
    return pl.pallas_call(
        kern,
        out_shape=jax.ShapeDtypeStruct((B, S, n_heads * dv), BF16),
        grid=(B, S // tq),
        in_specs=[pl.BlockSpec((1, n_heads, tq, c), lambda b, i: (b, 0, i, 0)),
                  pl.BlockSpec((1, S, c), lambda b, i: (b, 0, 0)),
                  pl.BlockSpec((1, 1, n_pieces, tq, LANES), lambda b, i: (b, i, 0, 0, 0)),
                  pl.BlockSpec((3, n_heads, LANES, LANES), lambda b, i: (0, 0, 0, 0)),
                  pl.BlockSpec((n_heads, c, dv), lambda b, i: (0, 0, 0))],
        out_specs=pl.BlockSpec((1, tq, n_heads * dv), lambda b, i: (b, i, 0)),
        scratch_shapes=[pltpu.VMEM((n_heads, tq, c), F32),
                        pltpu.VMEM((n_heads, tq, LANES), F32),
                        pltpu.VMEM((n_heads, tq, LANES), F32),
                        pltpu.VMEM((unroll * tq, tk), F32),
                        pltpu.VMEM((unroll * tq, tk), F32),
                        pltpu.VMEM((unroll, tq, LANES), F32),
                        pltpu.VMEM((unroll, tq, LANES), F32),
                        pltpu.VMEM((unroll * tq, tk), BF16),
                        pltpu.VMEM((unroll, tq, LANES), F32)],
        compiler_params=_params(("parallel", "parallel")),
        name="dsa_attention",
    )(q_abs, c_kv, mask, bias_tab, w_uv)


def _sb_attn_kernel(q_ref, k_ref, v_ref, o_ref, acc_ref, rest_ref, *, dh):
    i = pl.program_id(2)
    tq = q_ref.shape[1]
    tk = tq
    nh = q_ref.shape[2] // dh
    below = (lax.broadcasted_iota(jnp.int32, (tq, tk), 1)
             < lax.broadcasted_iota(jnp.int32, (tq, tk), 0))
    from_s = (lax.broadcasted_iota(jnp.int32, (2 * tk, tk), 0) % tk
              >= lax.broadcasted_iota(jnp.int32, (2 * tk, tk), 1)).astype(BF16)

    acc_ref[...] = jnp.zeros_like(acc_ref)
    rest_ref[...] = jnp.zeros_like(rest_ref)


    def block(j, diag):
        k0 = pl.multiple_of(j * tk, tk)
        hds = [slice(u * dh, (u + 1) * dh) for u in range(nh)]
        z2s = [lax.dot_general(q_ref[0, :, hd], k_ref[0, pl.ds(k0, tk), hd], NT_DIMS, preferred_element_type=F32)
               for hd in hds]
        log_rests, parts = [], []
        for z2 in z2s:
            neg_abs = lax.bitcast_convert_type(
                lax.bitcast_convert_type(z2, jnp.uint32) | jnp.uint32(0x80000000), F32)
            log_rest = jnp.log(1.0 + jnp.exp2(neg_abs)) * (-LOG2E) - jnp.maximum(z2, 0.0)
            if diag:
                log_rest = jnp.where(below, log_rest, 0.0)
            hi = lax.bitcast_convert_type(
                lax.bitcast_convert_type(log_rest, jnp.uint32) & jnp.uint32(0xFFFF0000), F32)
            parts.append(jnp.concatenate([hi.astype(BF16), (log_rest - hi).astype(BF16)], axis=1))
            log_rests.append(log_rest)
        cums = [jnp.dot(part, from_s, preferred_element_type=F32) for part in parts]
        wgts = []
        for u in range(nh):
            rest = rest_ref[u]
            wgt = jnp.exp2(z2s[u] + cums[u] + jnp.concatenate([rest] * (tk // LANES), axis=1))
            if diag:
                wgt = jnp.where(below, wgt, 0.0)
            wgts.append(wgt.astype(BF16))
            rest_ref[u] = rest + jnp.sum(log_rests[u], axis=1, keepdims=True)
        pvs = [jnp.dot(wgt, v_ref[0, pl.ds(k0, tk), hd], preferred_element_type=F32) for wgt, hd in zip(wgts, hds)]
        for pv, hd in zip(pvs, hds):
            acc_ref[:, hd] += pv

    def all_underflowed():
        worst = jnp.max(jnp.max(rest_ref[...], axis=0))
        return (worst <= -SB_UNDERFLOW_LOG2).astype(jnp.int32)

    block(i, True)

    def more(carry):
        jj, done = carry
        return jnp.logical_and(jj <= i, done == 0)

    def step(carry):
        jj, _ = carry
        block(i - jj, False)
        return jj + 1, all_underflowed()

    lax.while_loop(more, step, (jnp.int32(1), all_underflowed()))
    o_ref[0] = acc_ref[...].astype(o_ref.dtype)


def _sb_attention(qkv, n_heads):
    B, S, three_d = qkv.shape
    dh = three_d // (3 * n_heads)
    tq = min(S, 2 * LANES)
    nh = min(n_heads, 4)
    n_groups = n_heads // nh
    kern = functools.partial(_sb_attn_kernel, dh=dh)
    return pl.pallas_call(
        kern,
        out_shape=jax.ShapeDtypeStruct((B, S, n_heads * dh), BF16),
        grid=(B, n_groups, S // tq),
        in_specs=[pl.BlockSpec((1, tq, nh * dh), lambda b, g, i: (b, i, g)),
                  pl.BlockSpec((1, S, nh * dh), lambda b, g, i: (b, 0, n_groups + g)),
                  pl.BlockSpec((1, S, nh * dh), lambda b, g, i: (b, 0, 2 * n_groups + g))],
        out_specs=pl.BlockSpec((1, tq, nh * dh), lambda b, g, i: (b, i, g)),
        scratch_shapes=[pltpu.VMEM((tq, nh * dh), F32), pltpu.VMEM((nh, tq, LANES), F32)],
        compiler_params=_params(("parallel", "parallel", "parallel")),
        name="sb_attention",
    )(qkv, qkv, qkv)


def kernel(x, c, rel_bias, ada_w, ada_b, norm_g, a_w_in, a_q_norm, a_kv_norm, a_idx_k_norm, a_w_qb, a_w_idx_qb,
           a_w_uk, a_w_uv, a_w_out, b_w_in, b_w_out, mlp_w1, mlp_w2):
    B, S, D = x.shape
    depth = ada_w.shape[0]
    n_mixers = 2

    mod = _ada_mod(c, ada_w, ada_b)
    modr = mod.reshape(depth * B * 6, 1, D)

    def mod_row(layer, chunk):
        return lambda b: (layer * B + b) * 6 + chunk

    rq, rkv, di = a_q_norm.shape[1], a_kv_norm.shape[1], a_idx_k_norm.shape[1]
    n_a_heads, _, a_qk_dim = a_w_uk.shape[1:]
    n_idx_heads = a_w_idx_qb.shape[2] // di
    k_top = min(IDX_TOPK, S // 4)
    bias_tab = _bias_tiles(rel_bias)

    for layer in range(depth):
        j = layer // n_mixers
        g = norm_g[layer]
        if layer % n_mixers == 0:
            w_in = a_w_in[j].astype(BF16)
            wwi = jnp.pad(w_in[:, rq + rkv + di:], ((0, 0), (0, LANES - n_idx_heads)))
            q_abs, q_idx, c_kv, k_idx, w_idx = _dsa_proj(
                x, g[0], modr, mod_row(layer, 0), mod_row(layer, 1),
                w_in[:, :rq], w_in[:, rq:rq + rkv], w_in[:, rq + rkv:rq + rkv + di], wwi,
                a_q_norm[j], a_kv_norm[j], a_idx_k_norm[j],
                a_w_qb[j].astype(BF16), a_w_idx_qb[j].astype(BF16),
                (jnp.swapaxes(a_w_uk[j], 1, 2) * (a_qk_dim ** -0.5 * LOG2E)).astype(BF16), n_idx_heads)
            mask = _indexer_mask(q_idx, k_idx, w_idx, k_top)
            y = _dsa_attention(q_abs, c_kv, mask, bias_tab, a_w_uv[j].astype(BF16))
            w_out = a_w_out[j].astype(BF16)
        else:
            d_sb = b_w_in.shape[2] // 3
            col_scale = jnp.where(jnp.arange(3 * d_sb) < d_sb, (d_sb // B_HEADS) ** -0.5 * LOG2E, 1.0)
            qkv = _in_proj(x, g[0], modr, mod_row(layer, 0), mod_row(layer, 1),
                           (b_w_in[j] * col_scale).astype(BF16), BF16)
            y = _sb_attention(qkv, B_HEADS)
            w_out = b_w_out[j].astype(BF16)
        x = _out_proj(y, w_out, x, g[1], modr, mod_row(layer, 2))
        x = _mlp(x, g[2], g[3], modr, mod_row(layer, 3), mod_row(layer, 4), mod_row(layer, 5),
                 mlp_w1[layer].astype(BF16), mlp_w2[layer].astype(BF16))
    return x
```
